```python
import math
import jax, jax.numpy as jnp
from jax import lax
import numpy as np

D_MODEL = 1024
BATCH = 8
SEQ = 2048
DEPTH = 1
DEC_BATCH = 128
DEC_SEQ = 1
PAST_LEN = 16384
PAGE_SIZE = 128

N_META = 16
D_MIX = D_MODEL
EPS = 1e-6
GLA_HEADS = 4
GLA_DV = D_MIX // 2 // GLA_HEADS
GLA_DK = GLA_DV // 2
GLA_GATE_RANK = 16
GLA_GATE_NORM = 16.0
GLA_CHUNK = 64
MLA_HEADS = 4
MLA_NOPE = 128
MLA_ROPE = 64
MLA_V = (D_MIX - GLA_HEADS * GLA_DV) // MLA_HEADS
Q_LORA = 384
KV_LORA = 256
ROPE_THETA = 10000.0
MLA_SCALE = (MLA_NOPE + MLA_ROPE) ** -0.5
Q_BLOCK = 128
D_FF = -(-8 * D_MODEL // (3 * 256)) * 256
IN_SIZES = (GLA_HEADS * GLA_DK, GLA_HEADS * GLA_DK, GLA_HEADS * GLA_DV, GLA_GATE_RANK,
            GLA_HEADS * GLA_DV, Q_LORA, KV_LORA, MLA_ROPE)
IN_WIDTH = sum(IN_SIZES)

kernel_name = 'hymba_gla_mla_decode_step'


def _rmsnorm(x, w):
    xf = x.astype(jnp.float32)
    y = xf * lax.rsqrt(jnp.mean(xf * xf, axis=-1, keepdims=True) + EPS)
    return (y * w.astype(jnp.float32)).astype(x.dtype)


def _rope(x, pos):
    half = MLA_ROPE // 2
    inv = ROPE_THETA ** (-jnp.arange(half, dtype=jnp.float32) / half)
    ang = pos.astype(jnp.float32)[:, None] * inv[None, :]
    cos, sin = jnp.cos(ang), jnp.sin(ang)
    xf = x.astype(jnp.float32)
    x1, x2 = xf[..., :half], xf[..., half:]
    return jnp.concatenate([x1 * cos - x2 * sin, x1 * sin + x2 * cos], axis=-1).astype(x.dtype)


def _project(n, pos, w_in, w_gk, b_gk, q_norm, kv_norm, w_uq, w_ukv):
    Bn, T, _ = n.shape
    idx = np.cumsum(IN_SIZES)[:-1].tolist()
    q, k, v, gr, g, cq, ckv, kpe = jnp.split(n @ w_in, idx, axis=-1)
    heads = lambda a, d: a.reshape(Bn, T, GLA_HEADS, d).transpose(0, 2, 1, 3)
    gq = heads(q, GLA_DK) * (GLA_DK ** -0.5)
    gk = heads(k, GLA_DK)
    gv = heads(v, GLA_DV)
    lg = heads(jax.nn.log_sigmoid((gr @ w_gk + b_gk).astype(jnp.float32)) / GLA_GATE_NORM, GLA_DK)
    qh = (_rmsnorm(cq, q_norm) @ w_uq).reshape(Bn, T, MLA_HEADS, MLA_NOPE + MLA_ROPE)
    w_uk = w_ukv.reshape(KV_LORA, MLA_HEADS, MLA_NOPE + MLA_V)[..., :MLA_NOPE]
    q_lat = jnp.einsum('bthn,rhn->bhtr', qh[..., :MLA_NOPE], w_uk)
    q_pe = _rope(qh[..., MLA_NOPE:].transpose(0, 2, 1, 3), pos)
    ckv = _rmsnorm(ckv, kv_norm)
    kpe = _rope(kpe, pos)
    return (gq, gk, gv, lg, g), (q_lat, q_pe, ckv, kpe)


def _gla_chunk(S, q, k, v, lg):
    q, k, v, lg = (a.astype(jnp.float32) for a in (q, k, v, lg))
    S = S.astype(jnp.float32)
    C = q.shape[2]
    b = jnp.cumsum(lg, axis=2)
    causal = jnp.tril(jnp.ones((C, C), bool))
    diff = b[:, :, :, None, :] - b[:, :, None, :, :]
    decay = jnp.exp(jnp.where(causal[None, None, :, :, None], diff, -jnp.inf))
    A = jnp.einsum('bhtd,bhsd,bhtsd->bhts', q, k, decay)
    o = A @ v + jnp.einsum('bhtd,bhde->bhte', q * jnp.exp(b), S)
    b_last = b[:, :, -1:, :]
    S_new = jnp.exp(b_last[:, :, 0, :])[..., None] * S + jnp.einsum('bhsd,bhse->bhde', k * jnp.exp(b_last - b), v)
    return S_new, o


def _gla_prompt(q, k, v, lg):
    Bn, H, L, _ = q.shape
    S0 = jnp.zeros((Bn, H, GLA_DK, GLA_DV), jnp.float32)
    S, o_meta = _gla_chunk(S0, q[:, :, :N_META], k[:, :, :N_META], v[:, :, :N_META], lg[:, :, :N_META])
    n_chunks = (L - N_META) // GLA_CHUNK

    def chunks(a):
        return jnp.moveaxis(a[:, :, N_META:].reshape(Bn, H, n_chunks, GLA_CHUNK, a.shape[-1]), 2, 0)

    S, o = lax.scan(lambda s, xs: _gla_chunk(s, *xs), S, tuple(chunks(a) for a in (q, k, v, lg)))
    o = jnp.moveaxis(o, 0, 2).reshape(Bn, H, L - N_META, GLA_DV)
    return jnp.concatenate([o_meta, o], axis=2), S


def _latent_scores(q_lat, q_pe, ck, kp):
    s = jnp.einsum('bhtr,bkr->bhtk', q_lat, ck) + jnp.einsum('bhtp,bkp->bhtk', q_pe, kp)
    return s.astype(jnp.float32) * MLA_SCALE


def _mla_prompt(q_lat, q_pe, ckv, kpe):
    Bn, H, L, R = q_lat.shape
    n_blk = -(-L // Q_BLOCK)
    Lp = n_blk * Q_BLOCK
    padt = lambda a, ax: jnp.pad(a, [(0, Lp - L) if i == ax else (0, 0) for i in range(a.ndim)])
    q_lat, q_pe = padt(q_lat, 2), padt(q_pe, 2)
    ckv, kpe = padt(ckv, 1), padt(kpe, 1)
    key_pos = jnp.arange(Lp)

    def block(i):
        s = i * Q_BLOCK
        ql = lax.dynamic_slice_in_dim(q_lat, s, Q_BLOCK, axis=2)
        qp = lax.dynamic_slice_in_dim(q_pe, s, Q_BLOCK, axis=2)
        qpos = s + jnp.arange(Q_BLOCK)
        sc = jnp.where(key_pos[None, :] <= qpos[:, None], _latent_scores(ql, qp, ckv, kpe), -jnp.inf)
        p = jax.nn.softmax(sc, axis=-1)
        return jnp.einsum('bhtk,bkr->bhtr', p, ckv.astype(jnp.float32))

    o = lax.map(block, jnp.arange(n_blk))
    return jnp.moveaxis(o, 0, 2).reshape(Bn, H, Lp, R)[:, :, :L]


def _online_update(carry, sc, ck):
    m, l, acc = carry
    m_new = jnp.maximum(m, sc.max(-1))
    corr = jnp.exp(m - m_new)
    p = jnp.exp(sc - m_new[..., None])
    acc = acc * corr[..., None] + jnp.einsum('bhtk,bkr->bhtr', p, ck.astype(jnp.float32))
    return (m_new, l * corr + p.sum(-1), acc)


def _mla_sample(q_lat, q_pe, ck_new, kp_new, cache_ckv, cache_kpe, page_table, layer):
    Bd, H, T, R = q_lat.shape

    def page_step(carry, pages):
        ck = cache_ckv[layer, pages]
        kp = cache_kpe[layer, pages]
        return _online_update(carry, _latent_scores(q_lat, q_pe, ck, kp), ck), None

    init = (jnp.full((Bd, H, T), -jnp.inf, jnp.float32), jnp.zeros((Bd, H, T), jnp.float32),
            jnp.zeros((Bd, H, T, R), jnp.float32))
    carry, _ = lax.scan(page_step, init, page_table.T)
    causal = jnp.tril(jnp.ones((T, T), bool))
    sc = jnp.where(causal, _latent_scores(q_lat, q_pe, ck_new, kp_new), -jnp.inf)
    m, l, acc = _online_update(carry, sc, ck_new)
    return acc / l[..., None]


def _merge(o_gla, g, o_lat, gla_norm, w_ukv, w_out, dtype):
    Bn, _, T, _ = o_gla.shape
    og = _rmsnorm(o_gla, gla_norm).transpose(0, 2, 1, 3).reshape(Bn, T, GLA_HEADS * GLA_DV)
    og = og * jax.nn.silu(g.astype(jnp.float32))
    w_uv = w_ukv.reshape(KV_LORA, MLA_HEADS, MLA_NOPE + MLA_V)[..., MLA_NOPE:]
    om = jnp.einsum('bhtr,rhv->bthv', o_lat, w_uv).reshape(Bn, T, MLA_HEADS * MLA_V)
    return (jnp.concatenate([og, om.astype(jnp.float32)], axis=-1) @ w_out).astype(dtype)


def _ffn_block(h, mix, norm_ffn, w_gate, w_up, w_down):
    h = h + mix
    n = _rmsnorm(h, norm_ffn)
    return h + (jax.nn.silu(n @ w_gate) * (n @ w_up)) @ w_down


def setup_inputs(seed: int = 0) -> dict:
    key = jax.random.key(seed)
    ks = jax.random.split(key, 24)
    f32 = jnp.float32
    nrm = lambda k, shape, s=1.0: jax.random.normal(k, shape, f32) * s
    n_pages = PAST_LEN // PAGE_SIZE
    n_used = DEC_BATCH * n_pages
    n_phys = (5 * n_used + 3) // 4
    page_table = jax.random.permutation(ks[0], n_phys)[:n_used].reshape(DEC_BATCH, n_pages).astype(jnp.int32)
    return {
        'x_prompt': nrm(ks[1], (BATCH, SEQ, D_MODEL)),
        'x_sample': nrm(ks[2], (DEC_BATCH, DEC_SEQ, D_MODEL)),
        'cache_ckv': nrm(ks[3], (DEPTH, n_phys, PAGE_SIZE, KV_LORA)),
        'cache_kpe': nrm(ks[4], (DEPTH, n_phys, PAGE_SIZE, MLA_ROPE)),
        'state_gla': nrm(ks[5], (DEPTH, DEC_BATCH, GLA_HEADS, GLA_DK, GLA_DV), 0.3),
        'page_table': page_table,
        'meta_tokens': nrm(ks[6], (N_META, D_MODEL)),
        'norm_mix': 1.0 + nrm(ks[7], (DEPTH, D_MODEL), 0.05),
        'w_in': nrm(ks[8], (DEPTH, D_MODEL, IN_WIDTH), D_MODEL ** -0.5),
        'w_gk': nrm(ks[9], (DEPTH, GLA_GATE_RANK, GLA_HEADS * GLA_DK), GLA_GATE_RANK ** -0.5),
        'b_gk': nrm(ks[10], (DEPTH, GLA_HEADS * GLA_DK), 0.1),
        'gla_norm': 1.0 + nrm(ks[11], (DEPTH, GLA_DV), 0.05),
        'q_norm': 1.0 + nrm(ks[12], (DEPTH, Q_LORA), 0.05),
        'kv_norm': 1.0 + nrm(ks[13], (DEPTH, KV_LORA), 0.05),
        'w_uq': nrm(ks[14], (DEPTH, Q_LORA, MLA_HEADS * (MLA_NOPE + MLA_ROPE)), Q_LORA ** -0.5),
        'w_ukv': nrm(ks[15], (DEPTH, KV_LORA, MLA_HEADS * (MLA_NOPE + MLA_V)), KV_LORA ** -0.5),
        'w_out': nrm(ks[16], (DEPTH, D_MIX, D_MODEL), D_MIX ** -0.5),
        'norm_ffn': 1.0 + nrm(ks[17], (DEPTH, D_MODEL), 0.05),
        'w_gate': nrm(ks[18], (DEPTH, D_MODEL, D_FF), D_MODEL ** -0.5),
        'w_up': nrm(ks[19], (DEPTH, D_MODEL, D_FF), D_MODEL ** -0.5),
        'w_down': nrm(ks[20], (DEPTH, D_FF, D_MODEL), D_FF ** -0.5),
        'norm_final': 1.0 + nrm(ks[21], (D_MODEL,), 0.05),
    }


def reference(x_prompt, x_sample, cache_ckv, cache_kpe, state_gla, page_table, meta_tokens,
              norm_mix, w_in, w_gk, b_gk, gla_norm, q_norm, kv_norm, w_uq, w_ukv, w_out,
              norm_ffn, w_gate, w_up, w_down, norm_final):
    Bp = x_prompt.shape[0]
    T = x_sample.shape[1]
    past = page_table.shape[1] * PAGE_SIZE
    meta = jnp.broadcast_to(meta_tokens.astype(x_prompt.dtype)[None], (Bp, N_META, D_MODEL))
    hp = jnp.concatenate([meta, x_prompt], axis=1)
    pos_p = jnp.arange(hp.shape[1])
    pos_s = past + jnp.arange(T)
    hs = x_sample
    ckv_p, kpe_p, st_p, ckv_s, kpe_s, st_s = [], [], [], [], [], []
    for layer in range(DEPTH):
        proj_w = (w_in[layer], w_gk[layer], b_gk[layer], q_norm[layer], kv_norm[layer], w_uq[layer], w_ukv[layer])
        n = _rmsnorm(hp, norm_mix[layer])
        (gq, gk, gv, lg, g), (ql, qp, ck, kp) = _project(n, pos_p, *proj_w)
        o_gla, S = _gla_prompt(gq, gk, gv, lg)
        o_lat = _mla_prompt(ql, qp, ck, kp)
        mix = _merge(o_gla, g, o_lat, gla_norm[layer], w_ukv[layer], w_out[layer], hp.dtype)
        hp = _ffn_block(hp, mix, norm_ffn[layer], w_gate[layer], w_up[layer], w_down[layer])
        ckv_p.append(ck)
        kpe_p.append(kp)
        st_p.append(S.astype(state_gla.dtype))
        n = _rmsnorm(hs, norm_mix[layer])
        (gq, gk, gv, lg, g), (ql, qp, ck, kp) = _project(n, pos_s, *proj_w)
        S, o_gla = _gla_chunk(state_gla[layer], gq, gk, gv, lg)
        o_lat = _mla_sample(ql, qp, ck, kp, cache_ckv, cache_kpe, page_table, layer)
        mix = _merge(o_gla, g, o_lat, gla_norm[layer], w_ukv[layer], w_out[layer], hs.dtype)
        hs = _ffn_block(hs, mix, norm_ffn[layer], w_gate[layer], w_up[layer], w_down[layer])
        ckv_s.append(ck)
        kpe_s.append(kp)
        st_s.append(S.astype(state_gla.dtype))
    y_prompt = _rmsnorm(hp[:, N_META:], norm_final)
    y_sample = _rmsnorm(hs, norm_final)
    return (y_prompt, y_sample, jnp.stack(ckv_p), jnp.stack(kpe_p), jnp.stack(st_p),
            jnp.stack(ckv_s), jnp.stack(kpe_s), jnp.stack(st_s))
```

```python
import functools

import jax
import jax.numpy as jnp
from jax import lax
from jax.experimental import pallas as pl
from jax.experimental.pallas import tpu as pltpu

F32 = jnp.float32
BF16 = jnp.bfloat16

D_MODEL = 1024
N_META = 16
EPS = 1e-6
GLA_HEADS = 4
GLA_DV = 128
GLA_DK = 64
GLA_GATE_RANK = 16
GLA_GATE_NORM = 16.0
MLA_HEADS = 4
MLA_NOPE = 128
MLA_ROPE = 64
MLA_V = 128
Q_LORA = 384
KV_LORA = 256
ROPE_THETA = 10000.0
MLA_SCALE = (MLA_NOPE + MLA_ROPE) ** -0.5
PAGE_SIZE = 128
D_FF = 2816

LANES = 128
GLA_STEP = 16
QK_WIDTH = KV_LORA + LANES
NEG_BIG = -1e30

_OFF_Q, _OFF_K, _OFF_V, _OFF_G, _OFF_CQ, _OFF_CKV, _OFF_TAIL, _IN_PACKED = (
    0, 256, 512, 1024, 1536, 1920, 2176, 2304)
_TAIL_GR = MLA_ROPE


def _rms(x, w):
    return x * lax.rsqrt(jnp.mean(x * x, axis=-1, keepdims=True) + EPS) * w


def _rope_tile(x, c, s):
    lane = lax.broadcasted_iota(jnp.int32, x.shape, 1)
    from_right = pltpu.roll(x, LANES - MLA_ROPE // 2, 1)
    from_left = pltpu.roll(x, MLA_ROPE // 2, 1)
    swapped = jnp.where(lane < MLA_ROPE // 2, from_right, from_left)
    return x * c + swapped * s


def _dot(a, b):
    return jnp.dot(a, b, preferred_element_type=F32)


def _dot_nt(a, b):
    return lax.dot_general(a, b, (((1,), (1,)), ((), ())), preferred_element_type=F32)


def _proj_kernel(x_ref, cos_ref, sin_ref, nmix_ref, win_ref, wgk_ref, bgk_ref, qn_ref, kvn_ref,
                 wuq_ref, wukt_ref,
                 gq_ref, gk_ref, gv_ref, lg_ref, g_ref, qcat_ref, kcat_ref, ckv_ref, kpe_ref):
    n = _rms(x_ref[...], nmix_ref[...]).astype(BF16)
    proj = _dot(n, win_ref[...])
    gq_ref[...] = proj[:, _OFF_Q:_OFF_K] * (GLA_DK ** -0.5)
    gk_ref[...] = proj[:, _OFF_K:_OFF_V]
    gv_ref[...] = proj[:, _OFF_V:_OFF_G]
    g_ref[...] = proj[:, _OFF_G:_OFF_CQ]
    tail = proj[:, _OFF_TAIL:_IN_PACKED]
    z = _dot(tail.astype(BF16), wgk_ref[...]) + bgk_ref[...]
    lg_ref[...] = jax.nn.log_sigmoid(z) * (1.0 / GLA_GATE_NORM)

    cos = cos_ref[...]
    sin = sin_ref[...]
    cq = _rms(proj[:, _OFF_CQ:_OFF_CKV], qn_ref[...]).astype(BF16)
    qh = _dot(cq, wuq_ref[...])
    for h in range(MLA_HEADS):
        q_nope = qh[:, h * MLA_NOPE:(h + 1) * MLA_NOPE].astype(BF16)
        q_lat = _dot(q_nope, wukt_ref[h])
        base = MLA_HEADS * MLA_NOPE + h * LANES
        q_pe = _rope_tile(qh[:, base:base + LANES], cos, sin)
        qcat_ref[h] = (jnp.concatenate([q_lat, q_pe], axis=-1) * MLA_SCALE).astype(BF16)

    ckv = _rms(proj[:, _OFF_CKV:_OFF_TAIL], kvn_ref[...])
    ckv_ref[...] = ckv
    k_pe = _rope_tile(tail, cos, sin)
    kpe_ref[...] = k_pe[:, :MLA_ROPE]
    kcat_ref[...] = jnp.concatenate([ckv, k_pe], axis=-1).astype(BF16)


def _full_spec(a):
    nd = a.ndim
    return pl.BlockSpec(a.shape, lambda *_: (0,) * nd)


def _project(x, cos, sin, w, tm):
    t = x.shape[0]
    n_pos_blocks = cos.shape[0] // tm
    row = lambda width: pl.BlockSpec((tm, width), lambda i: (i, 0))
    pos = pl.BlockSpec((tm, LANES), lambda i: (i % n_pos_blocks, 0))
    weights = (w["norm_mix"], w["w_in"], w["w_gk"], w["b_gk"], w["q_norm"], w["kv_norm"],
               w["w_uq"], w["w_ukt"])
    out_shape = (
        jax.ShapeDtypeStruct((t, 256), F32), jax.ShapeDtypeStruct((t, 256), F32),
        jax.ShapeDtypeStruct((t, 512), F32), jax.ShapeDtypeStruct((t, 256), F32),
        jax.ShapeDtypeStruct((t, 512), F32),
        jax.ShapeDtypeStruct((MLA_HEADS, t, QK_WIDTH), BF16),
        jax.ShapeDtypeStruct((t, QK_WIDTH), BF16),
        jax.ShapeDtypeStruct((t, KV_LORA), F32), jax.ShapeDtypeStruct((t, MLA_ROPE), F32))
    out_specs = (row(256), row(256), row(512), row(256), row(512),
                 pl.BlockSpec((MLA_HEADS, tm, QK_WIDTH), lambda i: (0, i, 0)),
                 row(QK_WIDTH), row(KV_LORA), row(MLA_ROPE))
    names = ("gq", "gk", "gv", "lg", "g", "qcat", "kcat", "ckv", "kpe")
    outs = pl.pallas_call(
        _proj_kernel,
        grid=(t // tm,),
        in_specs=[row(D_MODEL), pos, pos] + [_full_spec(a) for a in weights],
        out_specs=out_specs,
        out_shape=out_shape,
        compiler_params=pltpu.CompilerParams(dimension_semantics=("parallel",)),
        name="project",
    )(x, cos, sin, *weights)
    return dict(zip(names, outs))


def _gla_kernel(q_ref, k_ref, lg_ref, v_ref, g_ref, gn_ref, s0_ref, og_ref, sout_ref, s_scr, *, n_sub):
    j = pl.program_id(1)

    @pl.when(j == 0)
    def _():
        s_scr[...] = s0_ref[0]

    c = GLA_STEP
    hk = GLA_HEADS * GLA_DK
    r16 = lax.broadcasted_iota(jnp.int32, (c, c), 0)
    c16 = lax.broadcasted_iota(jnp.int32, (c, c), 1)
    tri = (r16 >= c16).astype(F32)
    lane_head = lax.broadcasted_iota(jnp.int32, (c, hk), 1) // GLA_DK
    half = lax.broadcasted_iota(jnp.int32, (c, LANES), 1) // GLA_DK
    row_t = lax.broadcasted_iota(jnp.int32, (c, hk), 0)
    gn = gn_ref[...]
    zeros_pad = jnp.zeros((LANES - GLA_HEADS * c, hk), F32)

    state = s_scr[...]
    for u in range(n_sub):
        sl = slice(u * c, (u + 1) * c)
        q = q_ref[sl, :]
        k = k_ref[sl, :]
        v = v_ref[sl, :]
        b = jnp.dot(tri, lg_ref[sl, :], precision=lax.Precision.HIGHEST,
                    preferred_element_type=F32)
        b_last = b[c - 1:c, :]

        qt = q * jnp.exp(b)
        qm = jnp.concatenate([jnp.where(lane_head == h, qt, 0.0) for h in range(GLA_HEADS)], axis=0)
        o_st = _dot(qm.astype(BF16), state.astype(BF16))
        o_heads = [o_st[h * c:(h + 1) * c] for h in range(GLA_HEADS)]

        for s in range(c):
            decay = jnp.exp(jnp.minimum(b - b[s:s + 1, :], 0.0))
            w = jnp.where(row_t >= s, q * k[s:s + 1, :] * decay, 0.0)
            for h in range(GLA_HEADS):
                tile = w[:, (h // 2) * LANES:(h // 2 + 1) * LANES]
                a = jnp.sum(jnp.where(half == h % 2, tile, 0.0), axis=1, keepdims=True)
                o_heads[h] = o_heads[h] + a * v[s:s + 1, h * GLA_DV:(h + 1) * GLA_DV]

        kt = k * jnp.exp(b_last - b)
        kbd = jnp.concatenate([jnp.where(lane_head == h, kt, 0.0) for h in range(GLA_HEADS)]
                              + [zeros_pad], axis=0)
        vst = jnp.concatenate([v[:, h * GLA_DV:(h + 1) * GLA_DV] for h in range(GLA_HEADS)]
                              + [jnp.zeros((LANES - GLA_HEADS * c, GLA_DV), F32)], axis=0)
        upd = _dot(kbd.T.astype(BF16), vst.astype(BF16))
        dec = jnp.broadcast_to(jnp.exp(b_last), (LANES, hk)).T
        state = dec * state + upd

        g = g_ref[sl, :]
        normed = jnp.concatenate([_rms(o, gn) for o in o_heads], axis=-1)
        og_ref[sl, :] = (normed * (g * jax.nn.sigmoid(g))).astype(BF16)

    s_scr[...] = state

    @pl.when(j == pl.num_programs(1) - 1)
    def _():
        sout_ref[0] = state


def _gla_scan(q, k, lg, v, g, gla_norm, s0, n_seq, tt):
    t = q.shape[0]
    n_tiles = t // n_seq // tt
    row = lambda width: pl.BlockSpec((tt, width), lambda b, j: (b * n_tiles + j, 0))
    s0_map = (lambda b, j: (b, 0, 0)) if s0.shape[0] == n_seq else (lambda b, j: (0, 0, 0))
    hk = GLA_HEADS * GLA_DK
    og, s_out = pl.pallas_call(
        functools.partial(_gla_kernel, n_sub=tt // GLA_STEP),
        grid=(n_seq, n_tiles),
        in_specs=[row(256), row(256), row(256), row(512), row(512), _full_spec(gla_norm),
                  pl.BlockSpec((1, hk, GLA_DV), s0_map)],
        out_specs=(row(512), pl.BlockSpec((1, hk, GLA_DV), lambda b, j: (b, 0, 0))),
        out_shape=(jax.ShapeDtypeStruct((t, 512), BF16),
                   jax.ShapeDtypeStruct((n_seq, hk, GLA_DV), F32)),
        scratch_shapes=[pltpu.VMEM((hk, GLA_DV), F32)],
        compiler_params=pltpu.CompilerParams(dimension_semantics=("parallel", "arbitrary")),
        name="gla_scan",
    )(q, k, lg, v, g, gla_norm, s0)
    return og, s_out


def _mla_prompt_kernel(q_ref, k_ref, km_ref, o_ref, m_scr, l_scr, acc_scr, *, tq):
    qi = pl.program_id(1)
    m_rows = MLA_HEADS * tq
    q = q_ref[...].reshape(m_rows, QK_WIDTH)

    km = km_ref[...]
    s = _dot_nt(q, km)
    col = lax.broadcasted_iota(jnp.int32, s.shape, 1)
    s = jnp.where(col < N_META, s, NEG_BIG)
    m0 = jnp.max(s, axis=-1, keepdims=True)
    p = jnp.exp(s - m0)
    m_scr[...] = m0
    l_scr[...] = jnp.sum(p, axis=-1, keepdims=True)
    acc_scr[...] = _dot(p.astype(BF16), km[:, :KV_LORA])

    def tile(kt, masked):
        k = k_ref[pl.ds(pl.multiple_of(kt * tq, tq), tq), :]
        s = _dot_nt(q, k)
        if masked:
            row = lax.broadcasted_iota(jnp.int32, s.shape, 0) % tq
            col = lax.broadcasted_iota(jnp.int32, s.shape, 1)
            s = jnp.where(col <= row, s, NEG_BIG)
        m_prev = m_scr[...]
        m_new = jnp.maximum(m_prev, jnp.max(s, axis=-1, keepdims=True))
        corr = jnp.exp(m_prev - m_new)
        p = jnp.exp(s - m_new)
        l_scr[...] = l_scr[...] * corr + jnp.sum(p, axis=-1, keepdims=True)
        acc_scr[...] = acc_scr[...] * corr + _dot(p.astype(BF16), k[:, :KV_LORA])
        m_scr[...] = m_new

    def body(kt, carry):
        tile(kt, False)
        return carry

    lax.fori_loop(0, qi, body, 0)
    tile(qi, True)
    o = acc_scr[...] * (1.0 / l_scr[...])
    o_ref[...] = o.reshape(MLA_HEADS, tq, KV_LORA).astype(BF16)


def _mla_prompt(qcat, kcat, kmeta, n_seq, tq):
    t = kcat.shape[0]
    seq = t // n_seq
    nq = seq // tq
    return pl.pallas_call(
        functools.partial(_mla_prompt_kernel, tq=tq),
        grid=(n_seq, nq),
        in_specs=[pl.BlockSpec((MLA_HEADS, tq, QK_WIDTH), lambda b, i: (0, b * nq + i, 0)),
                  pl.BlockSpec((seq, QK_WIDTH), lambda b, i: (b, 0)),
                  _full_spec(kmeta)],
        out_specs=pl.BlockSpec((MLA_HEADS, tq, KV_LORA), lambda b, i: (0, b * nq + i, 0)),
        out_shape=jax.ShapeDtypeStruct((MLA_HEADS, t, KV_LORA), BF16),
        scratch_shapes=[pltpu.VMEM((MLA_HEADS * tq, 1), F32), pltpu.VMEM((MLA_HEADS * tq, 1), F32),
                        pltpu.VMEM((MLA_HEADS * tq, KV_LORA), F32)],
        compiler_params=pltpu.CompilerParams(dimension_semantics=("parallel", "arbitrary")),
        name="mla_prompt",
    )(qcat, kcat, kmeta)


Q_ROWS = 8


def _mla_decode_kernel(pt_ref, q_ref, kn_ref, cn_ref, ckv_hbm, kpe_hbm, o_ref,
                       ckv_buf, kpe_buf, sem, m_scr, l_scr, acc_scr, *, pg):
    s_id = pl.program_id(0)
    c = pl.program_id(1)
    nc = pl.num_programs(1)
    total = pl.num_programs(0) * nc
    step = s_id * nc + c
    slot = step % 2

    def page_copies(seq_i, chunk_i, slot_i):
        out = []
        for i in range(pg):
            page = pt_ref[seq_i, chunk_i * pg + i]
            out.append(pltpu.make_async_copy(ckv_hbm.at[page], ckv_buf.at[slot_i, i], sem.at[0, slot_i]))
            out.append(pltpu.make_async_copy(kpe_hbm.at[page], kpe_buf.at[slot_i, i], sem.at[1, slot_i]))
        return out

    @pl.when(step == 0)
    def _():
        for cp in page_copies(s_id, c, slot):
            cp.start()

    nxt = step + 1

    @pl.when(nxt < total)
    def _():
        for cp in page_copies(nxt // nc, nxt % nc, 1 - slot):
            cp.start()

    for cp in page_copies(s_id, c, slot):
        cp.wait()

    @pl.when(c == 0)
    def _():
        m_scr[...] = jnp.full(m_scr.shape, NEG_BIG, F32)
        l_scr[...] = jnp.zeros(l_scr.shape, F32)
        acc_scr[...] = jnp.zeros(acc_scr.shape, F32)

    q = q_ref[0]
    ckv = ckv_buf[slot].reshape(pg * PAGE_SIZE, KV_LORA).astype(BF16)
    kpe = kpe_buf[slot].reshape(pg * PAGE_SIZE, MLA_ROPE).astype(BF16)
    s = _dot_nt(q[:, :KV_LORA], ckv) + _dot_nt(q[:, KV_LORA:KV_LORA + MLA_ROPE], kpe)
    m_prev = m_scr[...]
    m_new = jnp.maximum(m_prev, jnp.max(s, axis=-1, keepdims=True))
    corr = jnp.exp(m_prev - m_new)
    p = jnp.exp(s - m_new)
    l_new = l_scr[...] * corr + jnp.sum(p, axis=-1, keepdims=True)
    acc_new = acc_scr[...] * corr + _dot(p.astype(BF16), ckv)
    m_scr[...] = m_new
    l_scr[...] = l_new
    acc_scr[...] = acc_new

    @pl.when(c == nc - 1)
    def _():
        s_self = jnp.sum(q.astype(F32) * kn_ref[0].astype(F32), axis=-1, keepdims=True)
        m_fin = jnp.maximum(m_new, s_self)
        corr_f = jnp.exp(m_new - m_fin)
        p_self = jnp.exp(s_self - m_fin)
        l_fin = l_new * corr_f + p_self
        acc_fin = acc_new * corr_f + p_self * cn_ref[0]
        o_ref[0] = acc_fin * (1.0 / l_fin)


def _mla_decode(page_table, qd, kn, cn, cache_ckv, cache_kpe, pg):
    n_seq, n_pages = page_table.shape
    grid_spec = pltpu.PrefetchScalarGridSpec(
        num_scalar_prefetch=1,
        grid=(n_seq, n_pages // pg),
        in_specs=[pl.BlockSpec((1, Q_ROWS, QK_WIDTH), lambda s, c, pt: (s, 0, 0)),
                  pl.BlockSpec((1, 1, QK_WIDTH), lambda s, c, pt: (s, 0, 0)),
                  pl.BlockSpec((1, 1, KV_LORA), lambda s, c, pt: (s, 0, 0)),
                  pl.BlockSpec(memory_space=pl.ANY),
                  pl.BlockSpec(memory_space=pl.ANY)],
        out_specs=pl.BlockSpec((1, Q_ROWS, KV_LORA), lambda s, c, pt: (s, 0, 0)),
        scratch_shapes=[pltpu.VMEM((2, pg, PAGE_SIZE, KV_LORA), F32),
                        pltpu.VMEM((2, pg, PAGE_SIZE, MLA_ROPE), F32),
                        pltpu.SemaphoreType.DMA((2, 2)),
                        pltpu.VMEM((Q_ROWS, 1), F32), pltpu.VMEM((Q_ROWS, 1), F32),
                        pltpu.VMEM((Q_ROWS, KV_LORA), F32)])
    return pl.pallas_call(
        functools.partial(_mla_decode_kernel, pg=pg),
        grid_spec=grid_spec,
        out_shape=jax.ShapeDtypeStruct((n_seq, Q_ROWS, KV_LORA), F32),
        compiler_params=pltpu.CompilerParams(dimension_semantics=("arbitrary", "arbitrary")),
        name="mla_decode",
    )(page_table, qd, kn, cn, cache_ckv, cache_kpe)


def _ffn_kernel(x_ref, og_ref, ol_ref, wuv_ref, wout_ref, nffn_ref, wg_ref, wu_ref, wd_ref, nfin_ref, y_ref):
    om = [_dot(ol_ref[h], wuv_ref[h]).astype(BF16) for h in range(MLA_HEADS)]
    cat = jnp.concatenate([og_ref[...]] + om, axis=-1)
    h1 = x_ref[...] + _dot(cat, wout_ref[...])
    n = _rms(h1, nffn_ref[...]).astype(BF16)
    gate = _dot(n, wg_ref[...])
    up = _dot(n, wu_ref[...])
    act = (gate * jax.nn.sigmoid(gate) * up).astype(BF16)
    h2 = h1 + _dot(act, wd_ref[...])
    y_ref[...] = _rms(h2, nfin_ref[...])


def _ffn(x, og, olat, w, tm):
    t = x.shape[0]
    row = lambda width: pl.BlockSpec((tm, width), lambda i: (i, 0))
    weights = (w["w_uv"], w["w_out"], w["norm_ffn"], w["w_gate"], w["w_up"], w["w_down"], w["norm_final"])
    return pl.pallas_call(
        _ffn_kernel,
        grid=(t // tm,),
        in_specs=[row(D_MODEL), row(512), pl.BlockSpec((MLA_HEADS, tm, KV_LORA), lambda i: (0, i, 0))]
                 + [_full_spec(a) for a in weights],
        out_specs=row(D_MODEL),
        out_shape=jax.ShapeDtypeStruct((t, D_MODEL), F32),
        compiler_params=pltpu.CompilerParams(dimension_semantics=("parallel",)),
        name="ffn",
    )(x, og, olat, *weights)


def _prep_weights(norm_mix, w_in, w_gk, b_gk, gla_norm, q_norm, kv_norm, w_uq, w_ukv, w_out,
                  norm_ffn, w_gate, w_up, w_down, norm_final):
    hk = GLA_HEADS * GLA_DK
    hv = GLA_HEADS * GLA_DV
    sizes = (hk, hk, hv, GLA_GATE_RANK, hv, Q_LORA, KV_LORA, MLA_ROPE)
    bounds = [0]
    for sz in sizes:
        bounds.append(bounds[-1] + sz)
    wq, wk, wv, wgr, wg, wcq, wckv, wkpe = (w_in[0][:, bounds[i]:bounds[i + 1]] for i in range(8))
    pad = jnp.zeros((D_MODEL, _IN_PACKED - _OFF_TAIL - MLA_ROPE - GLA_GATE_RANK), w_in.dtype)
    w_in_p = jnp.concatenate([wq, wk, wv, wg, wcq, wckv, wkpe, wgr, pad], axis=1).astype(BF16)

    w_gk_p = jnp.zeros((LANES, hk), F32).at[_TAIL_GR:_TAIL_GR + GLA_GATE_RANK].set(w_gk[0]).astype(BF16)

    wuq = w_uq[0].reshape(Q_LORA, MLA_HEADS, MLA_NOPE + MLA_ROPE)
    wuq_nope = wuq[..., :MLA_NOPE].reshape(Q_LORA, MLA_HEADS * MLA_NOPE)
    wuq_rope = jnp.pad(wuq[..., MLA_NOPE:], ((0, 0), (0, 0), (0, LANES - MLA_ROPE)))
    w_uq_p = jnp.concatenate([wuq_nope, wuq_rope.reshape(Q_LORA, MLA_HEADS * LANES)], axis=1).astype(BF16)

    wukv = w_ukv[0].reshape(KV_LORA, MLA_HEADS, MLA_NOPE + MLA_V)
    w_ukt = jnp.transpose(wukv[..., :MLA_NOPE], (1, 2, 0)).astype(BF16)
    w_uv = jnp.transpose(wukv[..., MLA_NOPE:], (1, 0, 2)).astype(BF16)

    r = lambda a: a.reshape(1, -1).astype(F32)
    return dict(norm_mix=r(norm_mix[0]), w_in=w_in_p, w_gk=w_gk_p, b_gk=r(b_gk[0]), q_norm=r(q_norm[0]),
                kv_norm=r(kv_norm[0]), w_uq=w_uq_p, w_ukt=w_ukt, w_uv=w_uv, gla_norm=r(gla_norm[0]),
                w_out=w_out[0].astype(BF16), norm_ffn=r(norm_ffn[0]), w_gate=w_gate[0].astype(BF16),
                w_up=w_up[0].astype(BF16), w_down=w_down[0].astype(BF16), norm_final=r(norm_final))


def _rope_tables(pos):
    half = MLA_ROPE // 2
    inv = ROPE_THETA ** (-jnp.arange(half, dtype=F32) / half)
    ang = pos.astype(F32)[:, None] * inv[None, :]
    cos, sin = jnp.cos(ang), jnp.sin(ang)
    z = jnp.zeros((pos.shape[0], LANES - MLA_ROPE), F32)
    return jnp.concatenate([cos, cos, z], axis=1), jnp.concatenate([-sin, sin, z], axis=1)


def kernel(x_prompt, x_sample, cache_ckv, cache_kpe, state_gla, page_table, meta_tokens, norm_mix, w_in, w_gk, b_gk, gla_norm, q_norm, kv_norm, w_uq, w_ukv, w_out, norm_ffn, w_gate, w_up, w_down, norm_final):
    n_b, seq, d = x_prompt.shape
    n_dec, t_dec, _ = x_sample.shape
    assert w_in.shape[0] == 1 and t_dec == 1 and d == D_MODEL
    n_pages = page_table.shape[1]
    past = n_pages * PAGE_SIZE
    hk = GLA_HEADS * GLA_DK
    w = _prep_weights(norm_mix, w_in, w_gk, b_gk, gla_norm, q_norm, kv_norm, w_uq, w_ukv, w_out,
                      norm_ffn, w_gate, w_up, w_down, norm_final)

    small_rows = 2 * LANES
    assert N_META <= LANES and n_dec == LANES
    xs = x_sample[:, 0]
    x_small = jnp.concatenate([meta_tokens.astype(F32), jnp.zeros((LANES - N_META, d), F32), xs], axis=0)
    pos_small = jnp.concatenate([jnp.arange(N_META), jnp.zeros((LANES - N_META,), jnp.int32),
                                 jnp.full((n_dec,), past, jnp.int32)])
    sm = _project(x_small, *_rope_tables(pos_small), w, small_rows)
    xp = x_prompt.reshape(n_b * seq, d)
    pr = _project(xp, *_rope_tables(N_META + jnp.arange(seq)), w, 256)

    gla_in = ("gq", "gk", "lg", "gv", "g")
    _, s_meta = _gla_scan(*(sm[n][:N_META] for n in gla_in), w["gla_norm"],
                          jnp.zeros((1, hk, GLA_DV), F32), 1, GLA_STEP)
    og_p, s_prompt = _gla_scan(*(pr[n] for n in gla_in), w["gla_norm"], s_meta, n_b, 128)
    pad_step = lambda a: jnp.pad(a[LANES:, None, :], ((0, 0), (0, GLA_STEP - 1), (0, 0))).reshape(
        n_dec * GLA_STEP, a.shape[-1])
    og_s, s_sample = _gla_scan(*(pad_step(sm[n]) for n in gla_in), w["gla_norm"],
                               state_gla[0].reshape(n_dec, hk, GLA_DV), n_dec, GLA_STEP)
    og_s = og_s[::GLA_STEP]

    kmeta = jnp.pad(sm["kcat"][:N_META], ((0, LANES - N_META), (0, 0)))
    olat_p = _mla_prompt(pr["qcat"], pr["kcat"], kmeta, n_b, 256)
    qd = jnp.pad(jnp.transpose(sm["qcat"][:, LANES:], (1, 0, 2)), ((0, 0), (0, Q_ROWS - MLA_HEADS), (0, 0)))
    o_dec = _mla_decode(page_table, qd, sm["kcat"][LANES:, None, :], sm["ckv"][LANES:, None, :],
                        cache_ckv[0], cache_kpe[0], 16)
    olat_s = jnp.transpose(o_dec[:, :MLA_HEADS], (1, 0, 2)).astype(BF16)

    y_prompt = _ffn(xp, og_p, olat_p, w, 256).reshape(n_b, seq, d)
    y_sample = _ffn(xs, og_s, olat_s, w, LANES).reshape(n_dec, 1, d)

    bcast = lambda a: jnp.broadcast_to(a[None, :N_META], (n_b, N_META, a.shape[-1]))
    ckv_prompt = jnp.concatenate([bcast(sm["ckv"]), pr["ckv"].reshape(n_b, seq, KV_LORA)], axis=1)[None]
    kpe_prompt = jnp.concatenate([bcast(sm["kpe"]), pr["kpe"].reshape(n_b, seq, MLA_ROPE)], axis=1)[None]
    gla_prompt = s_prompt.reshape(1, n_b, GLA_HEADS, GLA_DK, GLA_DV)
    ckv_sample = sm["ckv"][LANES:].reshape(1, n_dec, 1, KV_LORA)
    kpe_sample = sm["kpe"][LANES:].reshape(1, n_dec, 1, MLA_ROPE)
    gla_sample = s_sample.reshape(1, n_dec, GLA_HEADS, GLA_DK, GLA_DV)
    return (y_prompt, y_sample, ckv_prompt, kpe_prompt, gla_prompt, ckv_sample, kpe_sample, gla_sample)
```

```python
import functools

import jax
import jax.numpy as jnp
from jax import lax
from jax.experimental import pallas as pl
from jax.experimental.pallas import tpu as pltpu

F32 = jnp.float32
BF16 = jnp.bfloat16

D_MODEL = 1024
N_META = 16
EPS = 1e-6
GLA_HEADS = 4
GLA_DV = 128
GLA_DK = 64
GLA_GATE_RANK = 16
GLA_GATE_NORM = 16.0
MLA_HEADS = 4
MLA_NOPE = 128
MLA_ROPE = 64
MLA_V = 128
Q_LORA = 384
KV_LORA = 256
ROPE_THETA = 10000.0
MLA_SCALE = (MLA_NOPE + MLA_ROPE) ** -0.5
PAGE_SIZE = 128
D_FF = 2816

LANES = 128
GLA_STEP = 16
GLA_CHUNK = 64
GLA_SAFE_LOG_DECAY = 40.0
QK_WIDTH = KV_LORA + LANES
NEG_BIG = -1e30
Q_PRESCALE = MLA_SCALE * 1.4426950408889634

_OFF_Q, _OFF_K, _OFF_V, _OFF_G, _OFF_CQ, _OFF_CKV, _OFF_TAIL, _IN_PACKED = (
    0, 256, 512, 1024, 1536, 1920, 2176, 2304)
_TAIL_GR = MLA_ROPE


def _rms(x, w):
    return x * lax.rsqrt(jnp.mean(x * x, axis=-1, keepdims=True) + EPS) * w


def _rope_tile(x, c, s):
    lane = lax.broadcasted_iota(jnp.int32, x.shape, 1)
    from_right = pltpu.roll(x, LANES - MLA_ROPE // 2, 1)
    from_left = pltpu.roll(x, MLA_ROPE // 2, 1)
    swapped = jnp.where(lane < MLA_ROPE // 2, from_right, from_left)
    return x * c + swapped * s


def _dot(a, b):
    return jnp.dot(a, b, preferred_element_type=F32)


def _dot_nt(a, b):
    return lax.dot_general(a, b, (((1,), (1,)), ((), ())), preferred_element_type=F32)


def _proj_kernel(x_ref, cos_ref, sin_ref, nmix_ref, win_ref, wgk_ref, bgk_ref, qn_ref, kvn_ref,
                 wuq_ref, wukt_ref,
                 gq_ref, gk_ref, gv_ref, lg_ref, g_ref, qcat_ref, kcat_ref, ckv_ref, kpe_ref):
    n = _rms(x_ref[...], nmix_ref[...]).astype(BF16)
    proj = _dot(n, win_ref[...])
    gq_ref[...] = proj[:, _OFF_Q:_OFF_K] * (GLA_DK ** -0.5)
    gk_ref[...] = proj[:, _OFF_K:_OFF_V]
    gv_ref[...] = proj[:, _OFF_V:_OFF_G]
    g_ref[...] = proj[:, _OFF_G:_OFF_CQ]
    tail = proj[:, _OFF_TAIL:_IN_PACKED]
    z = _dot(tail.astype(BF16), wgk_ref[...]) + bgk_ref[...]
    lg_ref[...] = jax.nn.log_sigmoid(z) * (1.0 / GLA_GATE_NORM)

    cos = cos_ref[...]
    sin = sin_ref[...]
    cq = _rms(proj[:, _OFF_CQ:_OFF_CKV], qn_ref[...]).astype(BF16)
    qh = _dot(cq, wuq_ref[...])
    for h in range(MLA_HEADS):
        q_nope = qh[:, h * MLA_NOPE:(h + 1) * MLA_NOPE].astype(BF16)
        q_lat = _dot(q_nope, wukt_ref[h])
        base = MLA_HEADS * MLA_NOPE + h * LANES
        q_pe = _rope_tile(qh[:, base:base + LANES], cos, sin)
        qcat_ref[h] = (jnp.concatenate([q_lat, q_pe], axis=-1) * Q_PRESCALE).astype(BF16)

    ckv = _rms(proj[:, _OFF_CKV:_OFF_TAIL], kvn_ref[...])
    ckv_ref[...] = ckv
    k_pe = _rope_tile(tail, cos, sin)
    kpe_ref[...] = k_pe[:, :MLA_ROPE]
    kcat_ref[...] = jnp.concatenate([ckv, k_pe], axis=-1).astype(BF16)


def _full_spec(a):
    nd = a.ndim
    return pl.BlockSpec(a.shape, lambda *_: (0,) * nd)


def _project(x, cos, sin, w, tm):
    t = x.shape[0]
    n_pos_blocks = cos.shape[0] // tm
    row = lambda width: pl.BlockSpec((tm, width), lambda i: (i, 0))
    pos = pl.BlockSpec((tm, LANES), lambda i: (i % n_pos_blocks, 0))
    weights = (w["norm_mix"], w["w_in"], w["w_gk"], w["b_gk"], w["q_norm"], w["kv_norm"],
               w["w_uq"], w["w_ukt"])
    out_shape = (
        jax.ShapeDtypeStruct((t, 256), F32), jax.ShapeDtypeStruct((t, 256), F32),
        jax.ShapeDtypeStruct((t, 512), F32), jax.ShapeDtypeStruct((t, 256), F32),
        jax.ShapeDtypeStruct((t, 512), F32),
        jax.ShapeDtypeStruct((MLA_HEADS, t, QK_WIDTH), BF16),
        jax.ShapeDtypeStruct((t, QK_WIDTH), BF16),
        jax.ShapeDtypeStruct((t, KV_LORA), F32), jax.ShapeDtypeStruct((t, MLA_ROPE), F32))
    out_specs = (row(256), row(256), row(512), row(256), row(512),
                 pl.BlockSpec((MLA_HEADS, tm, QK_WIDTH), lambda i: (0, i, 0)),
                 row(QK_WIDTH), row(KV_LORA), row(MLA_ROPE))
    names = ("gq", "gk", "gv", "lg", "g", "qcat", "kcat", "ckv", "kpe")
    outs = pl.pallas_call(
        _proj_kernel,
        grid=(t // tm,),
        in_specs=[row(D_MODEL), pos, pos] + [_full_spec(a) for a in weights],
        out_specs=out_specs,
        out_shape=out_shape,
        compiler_params=pltpu.CompilerParams(dimension_semantics=("parallel",)),
        name="project",
    )(x, cos, sin, *weights)
    return dict(zip(names, outs))


def _cum_log_decay(lg):
    n = lg.shape[0]
    tri = (lax.broadcasted_iota(jnp.int32, (n, n), 0) >= lax.broadcasted_iota(jnp.int32, (n, n), 1)).astype(F32)
    return jnp.dot(tri, lg, precision=lax.Precision.HIGHEST, preferred_element_type=F32)


def _head_masked_rows(x, pad_rows=0):
    lane_head = lax.broadcasted_iota(jnp.int32, x.shape, 1) // GLA_DK
    blocks = [jnp.where(lane_head == h, x, 0.0) for h in range(GLA_HEADS)]
    if pad_rows:
        blocks.append(jnp.zeros((pad_rows, x.shape[1]), x.dtype))
    return jnp.concatenate(blocks, axis=0)


def _head_stacked_values(v, pad_rows=0):
    blocks = [v[:, h * GLA_DV:(h + 1) * GLA_DV] for h in range(GLA_HEADS)]
    if pad_rows:
        blocks.append(jnp.zeros((pad_rows, GLA_DV), v.dtype))
    return jnp.concatenate(blocks, axis=0)


def _gla_state_update(state, kt, v, b_last, pad_rows=0):
    kbd = _head_masked_rows(kt, pad_rows)
    upd = _dot(kbd.T.astype(BF16), _head_stacked_values(v, pad_rows).astype(BF16))
    dec = jnp.broadcast_to(jnp.exp(b_last), (LANES, kt.shape[1])).T
    return dec * state + upd


def _gla_gate_out(o_heads, g, gn):
    normed = jnp.concatenate([_rms(o, gn) for o in o_heads], axis=-1)
    return (normed * (g * jax.nn.sigmoid(g))).astype(BF16)


def _gla_exact_step(state, q, k, lg, v):
    c = GLA_STEP
    half = lax.broadcasted_iota(jnp.int32, (c, LANES), 1) // GLA_DK
    row_t = lax.broadcasted_iota(jnp.int32, q.shape, 0)
    b = _cum_log_decay(lg)
    b_last = b[c - 1:c, :]

    o_st = _dot(_head_masked_rows(q * jnp.exp(b)).astype(BF16), state.astype(BF16))
    o_heads = [o_st[h * c:(h + 1) * c] for h in range(GLA_HEADS)]

    for s in range(c):
        decay = jnp.exp(jnp.minimum(b - b[s:s + 1, :], 0.0))
        w = jnp.where(row_t >= s, q * k[s:s + 1, :] * decay, 0.0)
        for h in range(GLA_HEADS):
            tile = w[:, (h // 2) * LANES:(h // 2 + 1) * LANES]
            a = jnp.sum(jnp.where(half == h % 2, tile, 0.0), axis=1, keepdims=True)
            o_heads[h] = o_heads[h] + a * v[s:s + 1, h * GLA_DV:(h + 1) * GLA_DV]

    state = _gla_state_update(state, k * jnp.exp(b_last - b), v, b_last, LANES - GLA_HEADS * c)
    return state, o_heads


def _gla_matmul_chunk(state, q, k, b, v):
    c = GLA_CHUNK
    b_last = b[c - 1:c, :]
    qm = _head_masked_rows(q * jnp.exp(b)).astype(BF16)
    k_inv = (k * jnp.exp(-b)).astype(BF16)
    a = _dot_nt(qm, k_inv)
    row_t = lax.broadcasted_iota(jnp.int32, a.shape, 0) % c
    col_s = lax.broadcasted_iota(jnp.int32, a.shape, 1)
    a = jnp.where(col_s <= row_t, a, 0.0).astype(BF16)
    o_st = _dot(qm, state.astype(BF16))
    vb = v.astype(BF16)
    o_heads = [o_st[h * c:(h + 1) * c] + _dot(a[h * c:(h + 1) * c], vb[:, h * GLA_DV:(h + 1) * GLA_DV])
               for h in range(GLA_HEADS)]
    state = _gla_state_update(state, k * jnp.exp(b_last - b), v, b_last)
    return state, o_heads


def _gla_kernel(q_ref, k_ref, lg_ref, v_ref, g_ref, gn_ref, s0_ref, og_ref, sout_ref, s_scr, *, tt):
    j = pl.program_id(1)

    @pl.when(j == 0)
    def _():
        s_scr[...] = s0_ref[0]

    gn = gn_ref[...]

    def exact_steps(start, n_steps):
        state = s_scr[...]
        for u in range(n_steps):
            sl = slice(start + u * GLA_STEP, start + (u + 1) * GLA_STEP)
            state, o_heads = _gla_exact_step(state, q_ref[sl, :], k_ref[sl, :], lg_ref[sl, :], v_ref[sl, :])
            og_ref[sl, :] = _gla_gate_out(o_heads, g_ref[sl, :], gn)
        s_scr[...] = state

    if tt % GLA_CHUNK == 0:
        for ci in range(tt // GLA_CHUNK):
            sl = slice(ci * GLA_CHUNK, (ci + 1) * GLA_CHUNK)
            b = _cum_log_decay(lg_ref[sl, :])
            mild = jnp.max(-b[GLA_CHUNK - 1:GLA_CHUNK, :]) <= GLA_SAFE_LOG_DECAY

            @pl.when(mild)
            def _():
                state, o_heads = _gla_matmul_chunk(s_scr[...], q_ref[sl, :], k_ref[sl, :], b, v_ref[sl, :])
                og_ref[sl, :] = _gla_gate_out(o_heads, g_ref[sl, :], gn)
                s_scr[...] = state

            @pl.when(jnp.logical_not(mild))
            def _():
                exact_steps(ci * GLA_CHUNK, GLA_CHUNK // GLA_STEP)
    else:
        exact_steps(0, tt // GLA_STEP)

    @pl.when(j == pl.num_programs(1) - 1)
    def _():
        sout_ref[0] = s_scr[...]


def _gla_scan(q, k, lg, v, g, gla_norm, s0, n_seq, tt):
    t = q.shape[0]
    n_tiles = t // n_seq // tt
    row = lambda width: pl.BlockSpec((tt, width), lambda b, j: (b * n_tiles + j, 0))
    s0_map = (lambda b, j: (b, 0, 0)) if s0.shape[0] == n_seq else (lambda b, j: (0, 0, 0))
    hk = GLA_HEADS * GLA_DK
    og, s_out = pl.pallas_call(
        functools.partial(_gla_kernel, tt=tt),
        grid=(n_seq, n_tiles),
        in_specs=[row(256), row(256), row(256), row(512), row(512), _full_spec(gla_norm),
                  pl.BlockSpec((1, hk, GLA_DV), s0_map)],
        out_specs=(row(512), pl.BlockSpec((1, hk, GLA_DV), lambda b, j: (b, 0, 0))),
        out_shape=(jax.ShapeDtypeStruct((t, 512), BF16),
                   jax.ShapeDtypeStruct((n_seq, hk, GLA_DV), F32)),
        scratch_shapes=[pltpu.VMEM((hk, GLA_DV), F32)],
        compiler_params=pltpu.CompilerParams(dimension_semantics=("parallel", "arbitrary")),
        name="gla_scan",
    )(q, k, lg, v, g, gla_norm, s0)
    return og, s_out


def _mla_prompt_kernel(q_ref, k_ref, km_ref, o_ref, m_scr, l_scr, acc_scr, *, tq):
    qi = pl.program_id(1)
    m_rows = MLA_HEADS * tq
    q = q_ref[...].reshape(m_rows, QK_WIDTH)

    wide = lambda a, n: jnp.concatenate([a] * n, axis=-1)

    km = km_ref[...]
    s = _dot_nt(q, km)
    col = lax.broadcasted_iota(jnp.int32, s.shape, 1)
    s = jnp.where(col < N_META, s, NEG_BIG)
    m0 = jnp.broadcast_to(jnp.max(s, axis=-1, keepdims=True), (m_rows, LANES))
    p = jnp.exp2(s - m0)
    m_scr[...] = m0
    l_scr[...] = jnp.broadcast_to(jnp.sum(p, axis=-1, keepdims=True), (m_rows, LANES))
    acc_scr[...] = _dot(p.astype(BF16), km[:, :KV_LORA])

    def tile(kt, masked):
        k = k_ref[pl.ds(pl.multiple_of(kt * tq, tq), tq), :]
        s = _dot_nt(q, k)
        if masked:
            row = lax.broadcasted_iota(jnp.int32, s.shape, 0) % tq
            col = lax.broadcasted_iota(jnp.int32, s.shape, 1)
            s = jnp.where(col <= row, s, NEG_BIG)
        m_prev = m_scr[...]
        m_new = jnp.maximum(m_prev, jnp.max(s, axis=-1, keepdims=True))
        corr = jnp.exp2(m_prev - m_new)
        p = jnp.exp2(s - wide(m_new, tq // LANES))
        l_scr[...] = l_scr[...] * corr + jnp.sum(p, axis=-1, keepdims=True)
        acc_scr[...] = acc_scr[...] * wide(corr, KV_LORA // LANES) + _dot(p.astype(BF16), k[:, :KV_LORA])
        m_scr[...] = m_new

    def body(kt, carry):
        tile(kt, False)
        return carry

    lax.fori_loop(0, qi, body, 0)
    tile(qi, True)
    o = acc_scr[...] * wide(1.0 / l_scr[...], KV_LORA // LANES)
    o_ref[...] = o.reshape(MLA_HEADS, tq, KV_LORA).astype(BF16)


def _mla_prompt(qcat, kcat, kmeta, n_seq, tq):
    t = kcat.shape[0]
    seq = t // n_seq
    nq = seq // tq
    return pl.pallas_call(
        functools.partial(_mla_prompt_kernel, tq=tq),
        grid=(n_seq, nq),
        in_specs=[pl.BlockSpec((MLA_HEADS, tq, QK_WIDTH), lambda b, i: (0, b * nq + i, 0)),
                  pl.BlockSpec((seq, QK_WIDTH), lambda b, i: (b, 0)),
                  _full_spec(kmeta)],
        out_specs=pl.BlockSpec((MLA_HEADS, tq, KV_LORA), lambda b, i: (0, b * nq + i, 0)),
        out_shape=jax.ShapeDtypeStruct((MLA_HEADS, t, KV_LORA), BF16),
        scratch_shapes=[pltpu.VMEM((MLA_HEADS * tq, LANES), F32), pltpu.VMEM((MLA_HEADS * tq, LANES), F32),
                        pltpu.VMEM((MLA_HEADS * tq, KV_LORA), F32)],
        compiler_params=pltpu.CompilerParams(dimension_semantics=("parallel", "arbitrary")),
        name="mla_prompt",
    )(qcat, kcat, kmeta)


Q_ROWS = 8
DEC_GROUP = 8


def _mla_decode_kernel(pt_ref, q_ref, qt_ref, kn_ref, cn_ref, ckv_hbm, kpe_hbm, o_ref,
                       ckv_buf, kpe_buf, sem, m_scr, l_scr, acc_scr, *, pg):
    s_id = pl.program_id(0)
    c = pl.program_id(1)
    nc = pl.num_programs(1)
    total = pl.num_programs(0) * nc
    step = s_id * nc + c
    slot = step % 2

    def page_copies(seq_i, chunk_i, slot_i):
        out = []
        for i in range(pg):
            page = pt_ref[seq_i, chunk_i * pg + i]
            out.append(pltpu.make_async_copy(ckv_hbm.at[page], ckv_buf.at[slot_i, i], sem.at[0, slot_i]))
            out.append(pltpu.make_async_copy(kpe_hbm.at[page], kpe_buf.at[slot_i, i], sem.at[1, slot_i]))
        return out

    @pl.when(step == 0)
    def _():
        for cp in page_copies(s_id, c, slot):
            cp.start()

    def wait_slot(slot_i):
        pltpu.make_async_copy(ckv_hbm.at[pl.ds(0, pg)], ckv_buf.at[slot_i], sem.at[0, slot_i]).wait()
        pltpu.make_async_copy(kpe_hbm.at[pl.ds(0, pg)], kpe_buf.at[slot_i], sem.at[1, slot_i]).wait()

    wait_slot(slot)

    nxt = jnp.minimum(step + 1, total - 1)
    for cp in page_copies(nxt // nc, nxt % nc, 1 - slot):
        cp.start()

    @pl.when(c == 0)
    def _():
        m_scr[...] = jnp.full(m_scr.shape, NEG_BIG, F32)
        l_scr[...] = jnp.zeros(l_scr.shape, F32)
        acc_scr[...] = jnp.zeros(acc_scr.shape, F32)

    q = q_ref[0]
    q_pe = q[:, KV_LORA:KV_LORA + MLA_ROPE]
    q_t = qt_ref[0]
    n_groups = pg // DEC_GROUP
    group_keys = DEC_GROUP * PAGE_SIZE
    ckv_groups, parts = [], []
    for g in range(n_groups):
        ckv_g = ckv_buf[slot, g * DEC_GROUP:(g + 1) * DEC_GROUP].reshape(group_keys, KV_LORA).astype(BF16)
        ckv_groups.append(ckv_g)
        s_lat_t = _dot(ckv_g, q_t)
        for i in range(DEC_GROUP):
            s_lat = s_lat_t[i * PAGE_SIZE:(i + 1) * PAGE_SIZE].T[:Q_ROWS]
            kpe_t = kpe_buf[slot, g * DEC_GROUP + i].astype(BF16)
            parts.append(s_lat + _dot(q_pe, kpe_t))
    s = jnp.concatenate(parts, axis=-1)
    m_prev = m_scr[...]
    m_new = jnp.maximum(m_prev, jnp.max(s, axis=-1, keepdims=True))
    corr = jnp.exp2(m_prev - m_new)
    p = jnp.exp2(s - m_new)
    l_new = l_scr[...] * corr + jnp.sum(p, axis=-1, keepdims=True)
    pb = p.astype(BF16)
    pv = _dot(pb[:, :group_keys], ckv_groups[0])
    for g in range(1, n_groups):
        pv = pv + _dot(pb[:, g * group_keys:(g + 1) * group_keys], ckv_groups[g])
    acc_new = acc_scr[...] * corr + pv
    m_scr[...] = m_new
    l_scr[...] = l_new
    acc_scr[...] = acc_new

    @pl.when(c == nc - 1)
    def _():
        s_self = jnp.sum(q.astype(F32) * kn_ref[0].astype(F32), axis=-1, keepdims=True)
        m_fin = jnp.maximum(m_new, s_self)
        corr_f = jnp.exp2(m_new - m_fin)
        p_self = jnp.exp2(s_self - m_fin)
        l_fin = l_new * corr_f + p_self
        acc_fin = acc_new * corr_f + p_self * cn_ref[0]
        o_ref[0] = acc_fin * (1.0 / l_fin)

    @pl.when(step == total - 1)
    def _():
        wait_slot(1 - slot)


def _mla_decode(page_table, qd, qt, kn, cn, cache_ckv, cache_kpe_t, pg):
    n_seq, n_pages = page_table.shape
    grid_spec = pltpu.PrefetchScalarGridSpec(
        num_scalar_prefetch=1,
        grid=(n_seq, n_pages // pg),
        in_specs=[pl.BlockSpec((1, Q_ROWS, QK_WIDTH), lambda s, c, pt: (s, 0, 0)),
                  pl.BlockSpec((1, KV_LORA, LANES), lambda s, c, pt: (s, 0, 0)),
                  pl.BlockSpec((1, 1, QK_WIDTH), lambda s, c, pt: (s, 0, 0)),
                  pl.BlockSpec((1, 1, KV_LORA), lambda s, c, pt: (s, 0, 0)),
                  pl.BlockSpec(memory_space=pl.ANY),
                  pl.BlockSpec(memory_space=pl.ANY)],
        out_specs=pl.BlockSpec((1, Q_ROWS, KV_LORA), lambda s, c, pt: (s, 0, 0)),
        scratch_shapes=[pltpu.VMEM((2, pg, PAGE_SIZE, KV_LORA), F32),
                        pltpu.VMEM((2, pg, MLA_ROPE, PAGE_SIZE), F32),
                        pltpu.SemaphoreType.DMA((2, 2)),
                        pltpu.VMEM((Q_ROWS, 1), F32), pltpu.VMEM((Q_ROWS, 1), F32),
                        pltpu.VMEM((Q_ROWS, KV_LORA), F32)])
    return pl.pallas_call(
        functools.partial(_mla_decode_kernel, pg=pg),
        grid_spec=grid_spec,
        out_shape=jax.ShapeDtypeStruct((n_seq, Q_ROWS, KV_LORA), F32),
        compiler_params=pltpu.CompilerParams(dimension_semantics=("arbitrary", "arbitrary")),
        name="mla_decode",
    )(page_table, qd, qt, kn, cn, cache_ckv, cache_kpe_t)


def _ffn_kernel(x_ref, og_ref, ol_ref, wuv_ref, wout_ref, nffn_ref, wg_ref, wu_ref, wd_ref, nfin_ref, y_ref):
    om = [_dot(ol_ref[h], wuv_ref[h]).astype(BF16) for h in range(MLA_HEADS)]
    cat = jnp.concatenate([og_ref[...]] + om, axis=-1)
    h1 = x_ref[...] + _dot(cat, wout_ref[...])
    n = _rms(h1, nffn_ref[...]).astype(BF16)
    gate = _dot(n, wg_ref[...])
    up = _dot(n, wu_ref[...])
    act = (gate * jax.nn.sigmoid(gate) * up).astype(BF16)
    h2 = h1 + _dot(act, wd_ref[...])
    y_ref[...] = _rms(h2, nfin_ref[...])


def _ffn(x, og, olat, w, tm):
    t = x.shape[0]
    row = lambda width: pl.BlockSpec((tm, width), lambda i: (i, 0))
    weights = (w["w_uv"], w["w_out"], w["norm_ffn"], w["w_gate"], w["w_up"], w["w_down"], w["norm_final"])
    return pl.pallas_call(
        _ffn_kernel,
        grid=(t // tm,),
        in_specs=[row(D_MODEL), row(512), pl.BlockSpec((MLA_HEADS, tm, KV_LORA), lambda i: (0, i, 0))]
                 + [_full_spec(a) for a in weights],
        out_specs=row(D_MODEL),
        out_shape=jax.ShapeDtypeStruct((t, D_MODEL), F32),
        compiler_params=pltpu.CompilerParams(dimension_semantics=("parallel",)),
        name="ffn",
    )(x, og, olat, *weights)


def _prep_weights(norm_mix, w_in, w_gk, b_gk, gla_norm, q_norm, kv_norm, w_uq, w_ukv, w_out,
                  norm_ffn, w_gate, w_up, w_down, norm_final):
    hk = GLA_HEADS * GLA_DK
    hv = GLA_HEADS * GLA_DV
    sizes = (hk, hk, hv, GLA_GATE_RANK, hv, Q_LORA, KV_LORA, MLA_ROPE)
    bounds = [0]
    for sz in sizes:
        bounds.append(bounds[-1] + sz)
    wq, wk, wv, wgr, wg, wcq, wckv, wkpe = (w_in[0][:, bounds[i]:bounds[i + 1]] for i in range(8))
    pad = jnp.zeros((D_MODEL, _IN_PACKED - _OFF_TAIL - MLA_ROPE - GLA_GATE_RANK), w_in.dtype)
    w_in_p = jnp.concatenate([wq, wk, wv, wg, wcq, wckv, wkpe, wgr, pad], axis=1).astype(BF16)

    w_gk_p = jnp.zeros((LANES, hk), F32).at[_TAIL_GR:_TAIL_GR + GLA_GATE_RANK].set(w_gk[0]).astype(BF16)

    wuq = w_uq[0].reshape(Q_LORA, MLA_HEADS, MLA_NOPE + MLA_ROPE)
    wuq_nope = wuq[..., :MLA_NOPE].reshape(Q_LORA, MLA_HEADS * MLA_NOPE)
    wuq_rope = jnp.pad(wuq[..., MLA_NOPE:], ((0, 0), (0, 0), (0, LANES - MLA_ROPE)))
    w_uq_p = jnp.concatenate([wuq_nope, wuq_rope.reshape(Q_LORA, MLA_HEADS * LANES)], axis=1).astype(BF16)

    wukv = w_ukv[0].reshape(KV_LORA, MLA_HEADS, MLA_NOPE + MLA_V)
    w_ukt = jnp.transpose(wukv[..., :MLA_NOPE], (1, 2, 0)).astype(BF16)
    w_uv = jnp.transpose(wukv[..., MLA_NOPE:], (1, 0, 2)).astype(BF16)

    r = lambda a: a.reshape(1, -1).astype(F32)
    return dict(norm_mix=r(norm_mix[0]), w_in=w_in_p, w_gk=w_gk_p, b_gk=r(b_gk[0]), q_norm=r(q_norm[0]),
                kv_norm=r(kv_norm[0]), w_uq=w_uq_p, w_ukt=w_ukt, w_uv=w_uv, gla_norm=r(gla_norm[0]),
                w_out=w_out[0].astype(BF16), norm_ffn=r(norm_ffn[0]), w_gate=w_gate[0].astype(BF16),
                w_up=w_up[0].astype(BF16), w_down=w_down[0].astype(BF16), norm_final=r(norm_final))


def _rope_tables(pos):
    half = MLA_ROPE // 2
    inv = ROPE_THETA ** (-jnp.arange(half, dtype=F32) / half)
    ang = pos.astype(F32)[:, None] * inv[None, :]
    cos, sin = jnp.cos(ang), jnp.sin(ang)
    z = jnp.zeros((pos.shape[0], LANES - MLA_ROPE), F32)
    return jnp.concatenate([cos, cos, z], axis=1), jnp.concatenate([-sin, sin, z], axis=1)


def kernel(x_prompt, x_sample, cache_ckv, cache_kpe, state_gla, page_table, meta_tokens, norm_mix, w_in, w_gk, b_gk, gla_norm, q_norm, kv_norm, w_uq, w_ukv, w_out, norm_ffn, w_gate, w_up, w_down, norm_final):
    n_b, seq, d = x_prompt.shape
    n_dec, t_dec, _ = x_sample.shape
    assert w_in.shape[0] == 1 and t_dec == 1 and d == D_MODEL
    n_pages = page_table.shape[1]
    past = n_pages * PAGE_SIZE
    hk = GLA_HEADS * GLA_DK
    w = _prep_weights(norm_mix, w_in, w_gk, b_gk, gla_norm, q_norm, kv_norm, w_uq, w_ukv, w_out,
                      norm_ffn, w_gate, w_up, w_down, norm_final)

    small_rows = 2 * LANES
    assert N_META <= LANES and n_dec == LANES
    xs = x_sample[:, 0]
    x_small = jnp.concatenate([meta_tokens.astype(F32), jnp.zeros((LANES - N_META, d), F32), xs], axis=0)
    pos_small = jnp.concatenate([jnp.arange(N_META), jnp.zeros((LANES - N_META,), jnp.int32),
                                 jnp.full((n_dec,), past, jnp.int32)])
    sm = _project(x_small, *_rope_tables(pos_small), w, small_rows)
    xp = x_prompt.reshape(n_b * seq, d)
    pr = _project(xp, *_rope_tables(N_META + jnp.arange(seq)), w, 256)

    gla_in = ("gq", "gk", "lg", "gv", "g")
    _, s_meta = _gla_scan(*(sm[n][:N_META] for n in gla_in), w["gla_norm"],
                          jnp.zeros((1, hk, GLA_DV), F32), 1, GLA_STEP)
    og_p, s_prompt = _gla_scan(*(pr[n] for n in gla_in), w["gla_norm"], s_meta, n_b, 128)
    pad_step = lambda a: jnp.pad(a[LANES:, None, :], ((0, 0), (0, GLA_STEP - 1), (0, 0))).reshape(
        n_dec * GLA_STEP, a.shape[-1])
    og_s, s_sample = _gla_scan(*(pad_step(sm[n]) for n in gla_in), w["gla_norm"],
                               state_gla[0].reshape(n_dec, hk, GLA_DV), n_dec, GLA_STEP)
    og_s = og_s[::GLA_STEP]

    kmeta = jnp.pad(sm["kcat"][:N_META], ((0, LANES - N_META), (0, 0)))
    olat_p = _mla_prompt(pr["qcat"], pr["kcat"], kmeta, n_b, 256)
    qd = jnp.pad(jnp.transpose(sm["qcat"][:, LANES:], (1, 0, 2)), ((0, 0), (0, Q_ROWS - MLA_HEADS), (0, 0)))
    qt = jnp.pad(jnp.transpose(sm["qcat"][:, LANES:, :KV_LORA], (1, 2, 0)),
                 ((0, 0), (0, 0), (0, LANES - MLA_HEADS)))
    cache_kpe_t = jnp.swapaxes(cache_kpe, 2, 3).reshape(-1, MLA_ROPE, PAGE_SIZE)
    o_dec = _mla_decode(page_table, qd, qt, sm["kcat"][LANES:, None, :], sm["ckv"][LANES:, None, :],
                        cache_ckv.reshape(-1, PAGE_SIZE, KV_LORA), cache_kpe_t, 32)
    olat_s = jnp.transpose(o_dec[:, :MLA_HEADS], (1, 0, 2)).astype(BF16)

    y_prompt = _ffn(xp, og_p, olat_p, w, 256).reshape(n_b, seq, d)
    y_sample = _ffn(xs, og_s, olat_s, w, LANES).reshape(n_dec, 1, d)

    bcast = lambda a: jnp.broadcast_to(a[None, :N_META], (n_b, N_META, a.shape[-1]))
    ckv_prompt = jnp.concatenate([bcast(sm["ckv"]), pr["ckv"].reshape(n_b, seq, KV_LORA)], axis=1)[None]
    kpe_prompt = jnp.concatenate([bcast(sm["kpe"]), pr["kpe"].reshape(n_b, seq, MLA_ROPE)], axis=1)[None]
    gla_prompt = s_prompt.reshape(1, n_b, GLA_HEADS, GLA_DK, GLA_DV)
    ckv_sample = sm["ckv"][LANES:].reshape(1, n_dec, 1, KV_LORA)
    kpe_sample = sm["kpe"][LANES:].reshape(1, n_dec, 1, MLA_ROPE)
    gla_sample = s_sample.reshape(1, n_dec, GLA_HEADS, GLA_DK, GLA_DV)
    return (y_prompt, y_sample, ckv_prompt, kpe_prompt, gla_prompt, ckv_sample, kpe_sample, gla_sample)
```

```python
import functools

import jax
import jax.numpy as jnp
from jax import lax
from jax.experimental import pallas as pl
from jax.experimental.pallas import tpu as pltpu

F32 = jnp.float32
BF16 = jnp.bfloat16

D_MODEL = 1024
N_META = 16
EPS = 1e-6
GLA_HEADS = 4
GLA_DV = 128
GLA_DK = 64
GLA_GATE_RANK = 16
GLA_GATE_NORM = 16.0
MLA_HEADS = 4
MLA_NOPE = 128
MLA_ROPE = 64
MLA_V = 128
Q_LORA = 384
KV_LORA = 256
ROPE_THETA = 10000.0
MLA_SCALE = (MLA_NOPE + MLA_ROPE) ** -0.5
PAGE_SIZE = 128
D_FF = 2816

LANES = 128
GLA_STEP = 16
GLA_CHUNK = 64
GLA_SAFE_LOG_DECAY = 40.0
QK_WIDTH = KV_LORA + LANES
NEG_BIG = -1e30
Q_PRESCALE = MLA_SCALE * 1.4426950408889634

_OFF_Q, _OFF_K, _OFF_V, _OFF_G, _OFF_CQ, _OFF_CKV, _OFF_TAIL, _IN_PACKED = (
    0, 256, 512, 1024, 1536, 1920, 2176, 2304)
_TAIL_GR = MLA_ROPE


def _rms(x, w):
    return x * lax.rsqrt(jnp.mean(x * x, axis=-1, keepdims=True) + EPS) * w


def _rope_tile(x, c, s):
    lane = lax.broadcasted_iota(jnp.int32, x.shape, 1)
    from_right = pltpu.roll(x, LANES - MLA_ROPE // 2, 1)
    from_left = pltpu.roll(x, MLA_ROPE // 2, 1)
    swapped = jnp.where(lane < MLA_ROPE // 2, from_right, from_left)
    return x * c + swapped * s


def _dot(a, b):
    return jnp.dot(a, b, preferred_element_type=F32)


def _dot_nt(a, b):
    return lax.dot_general(a, b, (((1,), (1,)), ((), ())), preferred_element_type=F32)


def _proj_kernel(x_ref, cos_ref, sin_ref, nmix_ref, win_ref, wgk_ref, bgk_ref, qn_ref, kvn_ref,
                 wuq_ref, wukt_ref,
                 gq_ref, gk_ref, gv_ref, lg_ref, g_ref, qcat_ref, kcat_ref, ckv_ref, kpe_ref):
    n = _rms(x_ref[...], nmix_ref[...]).astype(BF16)
    proj = _dot(n, win_ref[...])
    gq_ref[...] = proj[:, _OFF_Q:_OFF_K] * (GLA_DK ** -0.5)
    gk_ref[...] = proj[:, _OFF_K:_OFF_V]
    gv_ref[...] = proj[:, _OFF_V:_OFF_G]
    g_ref[...] = proj[:, _OFF_G:_OFF_CQ]
    tail = proj[:, _OFF_TAIL:_IN_PACKED]
    z = _dot(tail.astype(BF16), wgk_ref[...]) + bgk_ref[...]
    lg_ref[...] = jax.nn.log_sigmoid(z) * (1.0 / GLA_GATE_NORM)

    cos = cos_ref[...]
    sin = sin_ref[...]
    cq = _rms(proj[:, _OFF_CQ:_OFF_CKV], qn_ref[...]).astype(BF16)
    qh = _dot(cq, wuq_ref[...])
    for h in range(MLA_HEADS):
        q_nope = qh[:, h * MLA_NOPE:(h + 1) * MLA_NOPE].astype(BF16)
        q_lat = _dot(q_nope, wukt_ref[h])
        base = MLA_HEADS * MLA_NOPE + h * LANES
        q_pe = _rope_tile(qh[:, base:base + LANES], cos, sin)
        qcat_ref[h] = (jnp.concatenate([q_lat, q_pe], axis=-1) * Q_PRESCALE).astype(BF16)

    ckv = _rms(proj[:, _OFF_CKV:_OFF_TAIL], kvn_ref[...])
    ckv_ref[...] = ckv
    k_pe = _rope_tile(tail, cos, sin)
    kpe_ref[...] = k_pe[:, :MLA_ROPE]
    kcat_ref[...] = jnp.concatenate([ckv, k_pe], axis=-1).astype(BF16)


def _full_spec(a):
    nd = a.ndim
    return pl.BlockSpec(a.shape, lambda *_: (0,) * nd)


def _project(x, cos, sin, w, tm):
    t = x.shape[0]
    n_pos_blocks = cos.shape[0] // tm
    row = lambda width: pl.BlockSpec((tm, width), lambda i: (i, 0))
    pos = pl.BlockSpec((tm, LANES), lambda i: (i % n_pos_blocks, 0))
    weights = (w["norm_mix"], w["w_in"], w["w_gk"], w["b_gk"], w["q_norm"], w["kv_norm"],
               w["w_uq"], w["w_ukt"])
    out_shape = (
        jax.ShapeDtypeStruct((t, 256), F32), jax.ShapeDtypeStruct((t, 256), F32),
        jax.ShapeDtypeStruct((t, 512), F32), jax.ShapeDtypeStruct((t, 256), F32),
        jax.ShapeDtypeStruct((t, 512), F32),
        jax.ShapeDtypeStruct((MLA_HEADS, t, QK_WIDTH), BF16),
        jax.ShapeDtypeStruct((t, QK_WIDTH), BF16),
        jax.ShapeDtypeStruct((t, KV_LORA), F32), jax.ShapeDtypeStruct((t, MLA_ROPE), F32))
    out_specs = (row(256), row(256), row(512), row(256), row(512),
                 pl.BlockSpec((MLA_HEADS, tm, QK_WIDTH), lambda i: (0, i, 0)),
                 row(QK_WIDTH), row(KV_LORA), row(MLA_ROPE))
    names = ("gq", "gk", "gv", "lg", "g", "qcat", "kcat", "ckv", "kpe")
    outs = pl.pallas_call(
        _proj_kernel,
        grid=(t // tm,),
        in_specs=[row(D_MODEL), pos, pos] + [_full_spec(a) for a in weights],
        out_specs=out_specs,
        out_shape=out_shape,
        compiler_params=pltpu.CompilerParams(dimension_semantics=("parallel",)),
        name="project",
    )(x, cos, sin, *weights)
    return dict(zip(names, outs))


def _cum_log_decay(lg):
    n = lg.shape[0]
    tri = (lax.broadcasted_iota(jnp.int32, (n, n), 0) >= lax.broadcasted_iota(jnp.int32, (n, n), 1)).astype(F32)
    return jnp.dot(tri, lg, precision=lax.Precision.HIGHEST, preferred_element_type=F32)


def _head_masked_rows(x, pad_rows=0):
    lane_head = lax.broadcasted_iota(jnp.int32, x.shape, 1) // GLA_DK
    blocks = [jnp.where(lane_head == h, x, 0.0) for h in range(GLA_HEADS)]
    if pad_rows:
        blocks.append(jnp.zeros((pad_rows, x.shape[1]), x.dtype))
    return jnp.concatenate(blocks, axis=0)


def _head_stacked_values(v, pad_rows=0):
    blocks = [v[:, h * GLA_DV:(h + 1) * GLA_DV] for h in range(GLA_HEADS)]
    if pad_rows:
        blocks.append(jnp.zeros((pad_rows, GLA_DV), v.dtype))
    return jnp.concatenate(blocks, axis=0)


def _gla_state_update(state, kt, v, b_last, pad_rows=0):
    kbd = _head_masked_rows(kt, pad_rows)
    upd = _dot(kbd.T.astype(BF16), _head_stacked_values(v, pad_rows).astype(BF16))
    dec = jnp.broadcast_to(jnp.exp(b_last), (LANES, kt.shape[1])).T
    return dec * state + upd


def _gla_gate_out(o_heads, g, gn):
    normed = jnp.concatenate([_rms(o, gn) for o in o_heads], axis=-1)
    return (normed * (g * jax.nn.sigmoid(g))).astype(BF16)


def _gla_exact_step(state, q, k, lg, v, n_src=None):
    c = q.shape[0]
    n_src = c if n_src is None else n_src
    half = lax.broadcasted_iota(jnp.int32, (c, LANES), 1) // GLA_DK
    row_t = lax.broadcasted_iota(jnp.int32, q.shape, 0)
    b = _cum_log_decay(lg)
    b_last = b[c - 1:c, :]

    o_st = _dot(_head_masked_rows(q * jnp.exp(b)).astype(BF16), state.astype(BF16))
    o_heads = [o_st[h * c:(h + 1) * c] for h in range(GLA_HEADS)]

    for s in range(n_src):
        decay = jnp.exp(jnp.minimum(b - b[s:s + 1, :], 0.0))
        w = jnp.where(row_t >= s, q * k[s:s + 1, :] * decay, 0.0)
        for h in range(GLA_HEADS):
            tile = w[:, (h // 2) * LANES:(h // 2 + 1) * LANES]
            a = jnp.sum(jnp.where(half == h % 2, tile, 0.0), axis=1, keepdims=True)
            o_heads[h] = o_heads[h] + a * v[s:s + 1, h * GLA_DV:(h + 1) * GLA_DV]

    state = _gla_state_update(state, k * jnp.exp(b_last - b), v, b_last, LANES - GLA_HEADS * c)
    return state, o_heads


def _gla_matmul_chunk(state, q, k, b, v):
    c = GLA_CHUNK
    b_last = b[c - 1:c, :]
    qm = _head_masked_rows(q * jnp.exp(b)).astype(BF16)
    k_inv = (k * jnp.exp(-b)).astype(BF16)
    a = _dot_nt(qm, k_inv)
    row_t = lax.broadcasted_iota(jnp.int32, a.shape, 0) % c
    col_s = lax.broadcasted_iota(jnp.int32, a.shape, 1)
    a = jnp.where(col_s <= row_t, a, 0.0).astype(BF16)
    o_st = _dot(qm, state.astype(BF16))
    vb = v.astype(BF16)
    o_heads = [o_st[h * c:(h + 1) * c] + _dot(a[h * c:(h + 1) * c], vb[:, h * GLA_DV:(h + 1) * GLA_DV])
               for h in range(GLA_HEADS)]
    state = _gla_state_update(state, k * jnp.exp(b_last - b), v, b_last)
    return state, o_heads


def _gla_kernel(q_ref, k_ref, lg_ref, v_ref, g_ref, gn_ref, s0_ref, og_ref, sout_ref, s_scr, *, tt):
    j = pl.program_id(1)

    @pl.when(j == 0)
    def _():
        s_scr[...] = s0_ref[0]

    gn = gn_ref[...]

    def exact_steps(start, n_steps):
        state = s_scr[...]
        for u in range(n_steps):
            sl = slice(start + u * GLA_STEP, start + (u + 1) * GLA_STEP)
            state, o_heads = _gla_exact_step(state, q_ref[sl, :], k_ref[sl, :], lg_ref[sl, :], v_ref[sl, :])
            og_ref[sl, :] = _gla_gate_out(o_heads, g_ref[sl, :], gn)
        s_scr[...] = state

    if tt % GLA_CHUNK == 0:
        for ci in range(tt // GLA_CHUNK):
            sl = slice(ci * GLA_CHUNK, (ci + 1) * GLA_CHUNK)
            b = _cum_log_decay(lg_ref[sl, :])
            mild = jnp.max(-b[GLA_CHUNK - 1:GLA_CHUNK, :]) <= GLA_SAFE_LOG_DECAY

            @pl.when(mild)
            def _():
                state, o_heads = _gla_matmul_chunk(s_scr[...], q_ref[sl, :], k_ref[sl, :], b, v_ref[sl, :])
                og_ref[sl, :] = _gla_gate_out(o_heads, g_ref[sl, :], gn)
                s_scr[...] = state

            @pl.when(jnp.logical_not(mild))
            def _():
                exact_steps(ci * GLA_CHUNK, GLA_CHUNK // GLA_STEP)
    else:
        exact_steps(0, tt // GLA_STEP)

    @pl.when(j == pl.num_programs(1) - 1)
    def _():
        sout_ref[0] = s_scr[...]


def _gla_scan(q, k, lg, v, g, gla_norm, s0, n_seq, tt):
    t = q.shape[0]
    n_tiles = t // n_seq // tt
    row = lambda width: pl.BlockSpec((tt, width), lambda b, j: (b * n_tiles + j, 0))
    s0_map = (lambda b, j: (b, 0, 0)) if s0.shape[0] == n_seq else (lambda b, j: (0, 0, 0))
    hk = GLA_HEADS * GLA_DK
    og, s_out = pl.pallas_call(
        functools.partial(_gla_kernel, tt=tt),
        grid=(n_seq, n_tiles),
        in_specs=[row(256), row(256), row(256), row(512), row(512), _full_spec(gla_norm),
                  pl.BlockSpec((1, hk, GLA_DV), s0_map)],
        out_specs=(row(512), pl.BlockSpec((1, hk, GLA_DV), lambda b, j: (b, 0, 0))),
        out_shape=(jax.ShapeDtypeStruct((t, 512), BF16),
                   jax.ShapeDtypeStruct((n_seq, hk, GLA_DV), F32)),
        scratch_shapes=[pltpu.VMEM((hk, GLA_DV), F32)],
        compiler_params=pltpu.CompilerParams(dimension_semantics=("parallel", "arbitrary")),
        name="gla_scan",
    )(q, k, lg, v, g, gla_norm, s0)
    return og, s_out


TOKEN_ROWS = 8
TOKEN_SEQS = 8


def _gla_token_kernel(q_ref, k_ref, lg_ref, v_ref, g_ref, gn_ref, s0_ref, og_ref, sout_ref):
    gn = gn_ref[...]
    for i in range(TOKEN_SEQS):
        sl = slice(i * TOKEN_ROWS, (i + 1) * TOKEN_ROWS)
        state, o_heads = _gla_exact_step(s0_ref[i], q_ref[sl, :], k_ref[sl, :], lg_ref[sl, :], v_ref[sl, :],
                                         n_src=1)
        og_ref[sl, :] = _gla_gate_out(o_heads, g_ref[sl, :], gn)
        sout_ref[i] = state


def _gla_token(q, k, lg, v, g, gla_norm, s0):
    n = q.shape[0]
    hk = GLA_HEADS * GLA_DK
    pad = lambda a: jnp.pad(a[:, None, :], ((0, 0), (0, TOKEN_ROWS - 1), (0, 0))).reshape(n * TOKEN_ROWS, -1)
    rows = TOKEN_SEQS * TOKEN_ROWS
    row = lambda width: pl.BlockSpec((rows, width), lambda i: (i, 0))
    st = pl.BlockSpec((TOKEN_SEQS, hk, GLA_DV), lambda i: (i, 0, 0))
    og, s_out = pl.pallas_call(
        _gla_token_kernel,
        grid=(n // TOKEN_SEQS,),
        in_specs=[row(256), row(256), row(256), row(512), row(512), _full_spec(gla_norm), st],
        out_specs=(row(512), st),
        out_shape=(jax.ShapeDtypeStruct((n * TOKEN_ROWS, 512), BF16),
                   jax.ShapeDtypeStruct((n, hk, GLA_DV), F32)),
        compiler_params=pltpu.CompilerParams(dimension_semantics=("parallel",)),
        name="gla_token",
    )(pad(q), pad(k), pad(lg), pad(v), pad(g), gla_norm, s0)
    return og[::TOKEN_ROWS], s_out


MLA_ROW_BLOCKS = 1


def _mla_prompt_kernel(q_ref, k_ref, km_ref, o_ref, m_scr, l_scr, acc_scr, *, tq):
    qi = pl.program_id(1)

    wide = lambda a, n: jnp.concatenate([a] * n, axis=-1)

    blk_heads = MLA_HEADS // MLA_ROW_BLOCKS
    blk_rows = blk_heads * tq
    blocks = [(slice(i * blk_heads, (i + 1) * blk_heads), slice(i * blk_rows, (i + 1) * blk_rows))
              for i in range(MLA_ROW_BLOCKS)]
    q_blk = lambda hs: q_ref[hs].reshape(blk_rows, QK_WIDTH)

    km = km_ref[...]
    for hs, rs in blocks:
        s = _dot_nt(q_blk(hs), km)
        col = lax.broadcasted_iota(jnp.int32, s.shape, 1)
        s = jnp.where(col < N_META, s, NEG_BIG)
        m0 = jnp.broadcast_to(jnp.max(s, axis=-1, keepdims=True), (blk_rows, LANES))
        p = jnp.exp2(s - m0)
        m_scr[rs, :] = m0
        l_scr[rs, :] = jnp.broadcast_to(jnp.sum(p, axis=-1, keepdims=True), (blk_rows, LANES))
        acc_scr[rs, :] = _dot(p.astype(BF16), km[:, :KV_LORA])

    def tile(kt, masked):
        k = k_ref[pl.ds(pl.multiple_of(kt * tq, tq), tq), :]
        for hs, rs in blocks:
            s = _dot_nt(q_blk(hs), k)
            if masked:
                row = lax.broadcasted_iota(jnp.int32, s.shape, 0) % tq
                col = lax.broadcasted_iota(jnp.int32, s.shape, 1)
                s = jnp.where(col <= row, s, NEG_BIG)
            m_prev = m_scr[rs, :]
            m_new = jnp.maximum(m_prev, jnp.max(s, axis=-1, keepdims=True))
            corr = jnp.exp2(m_prev - m_new)
            p = jnp.exp2(s - wide(m_new, tq // LANES))
            l_scr[rs, :] = l_scr[rs, :] * corr + jnp.sum(p, axis=-1, keepdims=True)
            acc_scr[rs, :] = (acc_scr[rs, :] * wide(corr, KV_LORA // LANES)
                              + _dot(p.astype(BF16), k[:, :KV_LORA]))
            m_scr[rs, :] = m_new

    def body(kt, carry):
        tile(kt, False)
        return carry

    lax.fori_loop(0, qi, body, 0)
    tile(qi, True)
    for hs, rs in blocks:
        o = acc_scr[rs, :] * wide(1.0 / l_scr[rs, :], KV_LORA // LANES)
        o_ref[hs] = o.reshape(blk_heads, tq, KV_LORA).astype(BF16)


def _mla_prompt(qcat, kcat, kmeta, n_seq, tq):
    t = kcat.shape[0]
    seq = t // n_seq
    nq = seq // tq
    return pl.pallas_call(
        functools.partial(_mla_prompt_kernel, tq=tq),
        grid=(n_seq, nq),
        in_specs=[pl.BlockSpec((MLA_HEADS, tq, QK_WIDTH), lambda b, i: (0, b * nq + i, 0)),
                  pl.BlockSpec((seq, QK_WIDTH), lambda b, i: (b, 0)),
                  _full_spec(kmeta)],
        out_specs=pl.BlockSpec((MLA_HEADS, tq, KV_LORA), lambda b, i: (0, b * nq + i, 0)),
        out_shape=jax.ShapeDtypeStruct((MLA_HEADS, t, KV_LORA), BF16),
        scratch_shapes=[pltpu.VMEM((MLA_HEADS * tq, LANES), F32), pltpu.VMEM((MLA_HEADS * tq, LANES), F32),
                        pltpu.VMEM((MLA_HEADS * tq, KV_LORA), F32)],
        compiler_params=pltpu.CompilerParams(dimension_semantics=("parallel", "arbitrary")),
        name="mla_prompt",
    )(qcat, kcat, kmeta)


Q_ROWS = 8
DEC_GROUP = 8


def _mla_decode_kernel(pt_ref, q_ref, qt_ref, kn_ref, cn_ref, ckv_hbm, kpe_hbm, o_ref,
                       ckv_buf, kpe_buf, sem, *, pg, nc, n_seq):
    s_id = pl.program_id(0)

    def start_chunk(seq_i, ci):
        for i in range(pg):
            page = pt_ref[seq_i, ci * pg + i]
            pltpu.make_async_copy(ckv_hbm.at[page], ckv_buf.at[ci, i], sem.at[0, ci]).start()
            pltpu.make_async_copy(kpe_hbm.at[page], kpe_buf.at[ci, i], sem.at[1, ci]).start()

    def wait_chunk(ci):
        pltpu.make_async_copy(ckv_hbm.at[pl.ds(0, pg)], ckv_buf.at[ci], sem.at[0, ci]).wait()
        pltpu.make_async_copy(kpe_hbm.at[pl.ds(0, pg)], kpe_buf.at[ci], sem.at[1, ci]).wait()

    @pl.when(s_id == 0)
    def _():
        for ci in range(nc):
            start_chunk(0, ci)

    nxt = jnp.minimum(s_id + 1, n_seq - 1)

    q = q_ref[0]
    q_pe = q[:, KV_LORA:KV_LORA + MLA_ROPE]
    q_t = qt_ref[0]
    group_keys = DEC_GROUP * PAGE_SIZE
    n_groups = pg // DEC_GROUP
    m = jnp.full((Q_ROWS, 1), NEG_BIG, F32)
    l = jnp.zeros((Q_ROWS, 1), F32)
    acc = jnp.zeros((Q_ROWS, KV_LORA), F32)
    for ci in range(nc):
        wait_chunk(ci)
        ckv_groups, parts = [], []
        for g in range(n_groups):
            ckv_g = ckv_buf[ci, g * DEC_GROUP:(g + 1) * DEC_GROUP].reshape(group_keys, KV_LORA).astype(BF16)
            ckv_groups.append(ckv_g)
            s_lat_t = _dot(ckv_g, q_t)
            for i in range(DEC_GROUP):
                s_lat = s_lat_t[i * PAGE_SIZE:(i + 1) * PAGE_SIZE].T[:Q_ROWS]
                kpe_t = kpe_buf[ci, g * DEC_GROUP + i].astype(BF16)
                parts.append(s_lat + _dot(q_pe, kpe_t))
        s = jnp.concatenate(parts, axis=-1)
        m_new = jnp.maximum(m, jnp.max(s, axis=-1, keepdims=True))
        corr = jnp.exp2(m - m_new)
        pb = jnp.exp2(s - m_new)
        l = l * corr + jnp.sum(pb, axis=-1, keepdims=True)
        pb = pb.astype(BF16)
        pv = _dot(pb[:, :group_keys], ckv_groups[0])
        for g in range(1, n_groups):
            pv = pv + _dot(pb[:, g * group_keys:(g + 1) * group_keys], ckv_groups[g])
        acc = acc * corr + pv
        m = m_new
        start_chunk(nxt, ci)

    s_self = jnp.sum(q.astype(F32) * kn_ref[0].astype(F32), axis=-1, keepdims=True)
    m_fin = jnp.maximum(m, s_self)
    corr = jnp.exp2(m - m_fin)
    p_self = jnp.exp2(s_self - m_fin)
    l_fin = l * corr + p_self
    o_ref[0] = (acc * corr + p_self * cn_ref[0]) * (1.0 / l_fin)

    @pl.when(s_id == n_seq - 1)
    def _():
        for ci in range(nc):
            wait_chunk(ci)


def _mla_decode(page_table, qd, qt, kn, cn, cache_ckv, cache_kpe_t, pg):
    n_seq, n_pages = page_table.shape
    nc = n_pages // pg
    grid_spec = pltpu.PrefetchScalarGridSpec(
        num_scalar_prefetch=1,
        grid=(n_seq,),
        in_specs=[pl.BlockSpec((1, Q_ROWS, QK_WIDTH), lambda s, pt: (s, 0, 0)),
                  pl.BlockSpec((1, KV_LORA, LANES), lambda s, pt: (s, 0, 0)),
                  pl.BlockSpec((1, 1, QK_WIDTH), lambda s, pt: (s, 0, 0)),
                  pl.BlockSpec((1, 1, KV_LORA), lambda s, pt: (s, 0, 0)),
                  pl.BlockSpec(memory_space=pl.ANY),
                  pl.BlockSpec(memory_space=pl.ANY)],
        out_specs=pl.BlockSpec((1, Q_ROWS, KV_LORA), lambda s, pt: (s, 0, 0)),
        scratch_shapes=[pltpu.VMEM((nc, pg, PAGE_SIZE, KV_LORA), F32),
                        pltpu.VMEM((nc, pg, MLA_ROPE, PAGE_SIZE), F32),
                        pltpu.SemaphoreType.DMA((2, nc))])
    return pl.pallas_call(
        functools.partial(_mla_decode_kernel, pg=pg, nc=nc, n_seq=n_seq),
        grid_spec=grid_spec,
        out_shape=jax.ShapeDtypeStruct((n_seq, Q_ROWS, KV_LORA), F32),
        compiler_params=pltpu.CompilerParams(dimension_semantics=("arbitrary",)),
        name="mla_decode",
    )(page_table, qd, qt, kn, cn, cache_ckv, cache_kpe_t)


def _ffn_kernel(x_ref, og_ref, ol_ref, wuv_ref, wout_ref, nffn_ref, wg_ref, wu_ref, wd_ref, nfin_ref, y_ref):
    om = [_dot(ol_ref[h], wuv_ref[h]).astype(BF16) for h in range(MLA_HEADS)]
    cat = jnp.concatenate([og_ref[...]] + om, axis=-1)
    h1 = x_ref[...] + _dot(cat, wout_ref[...])
    n = _rms(h1, nffn_ref[...]).astype(BF16)
    gate = _dot(n, wg_ref[...])
    up = _dot(n, wu_ref[...])
    act = (gate * jax.nn.sigmoid(gate) * up).astype(BF16)
    h2 = h1 + _dot(act, wd_ref[...])
    y_ref[...] = _rms(h2, nfin_ref[...])


def _ffn(x, og, olat, w, tm):
    t = x.shape[0]
    row = lambda width: pl.BlockSpec((tm, width), lambda i: (i, 0))
    weights = (w["w_uv"], w["w_out"], w["norm_ffn"], w["w_gate"], w["w_up"], w["w_down"], w["norm_final"])
    return pl.pallas_call(
        _ffn_kernel,
        grid=(t // tm,),
        in_specs=[row(D_MODEL), row(512), pl.BlockSpec((MLA_HEADS, tm, KV_LORA), lambda i: (0, i, 0))]
                 + [_full_spec(a) for a in weights],
        out_specs=row(D_MODEL),
        out_shape=jax.ShapeDtypeStruct((t, D_MODEL), F32),
        compiler_params=pltpu.CompilerParams(dimension_semantics=("parallel",)),
        name="ffn",
    )(x, og, olat, *weights)


def _prep_weights(norm_mix, w_in, w_gk, b_gk, gla_norm, q_norm, kv_norm, w_uq, w_ukv, w_out,
                  norm_ffn, w_gate, w_up, w_down, norm_final):
    hk = GLA_HEADS * GLA_DK
    hv = GLA_HEADS * GLA_DV
    sizes = (hk, hk, hv, GLA_GATE_RANK, hv, Q_LORA, KV_LORA, MLA_ROPE)
    bounds = [0]
    for sz in sizes:
        bounds.append(bounds[-1] + sz)
    wq, wk, wv, wgr, wg, wcq, wckv, wkpe = (w_in[0][:, bounds[i]:bounds[i + 1]] for i in range(8))
    pad = jnp.zeros((D_MODEL, _IN_PACKED - _OFF_TAIL - MLA_ROPE - GLA_GATE_RANK), w_in.dtype)
    w_in_p = jnp.concatenate([wq, wk, wv, wg, wcq, wckv, wkpe, wgr, pad], axis=1).astype(BF16)

    w_gk_p = jnp.zeros((LANES, hk), F32).at[_TAIL_GR:_TAIL_GR + GLA_GATE_RANK].set(w_gk[0]).astype(BF16)

    wuq = w_uq[0].reshape(Q_LORA, MLA_HEADS, MLA_NOPE + MLA_ROPE)
    wuq_nope = wuq[..., :MLA_NOPE].reshape(Q_LORA, MLA_HEADS * MLA_NOPE)
    wuq_rope = jnp.pad(wuq[..., MLA_NOPE:], ((0, 0), (0, 0), (0, LANES - MLA_ROPE)))
    w_uq_p = jnp.concatenate([wuq_nope, wuq_rope.reshape(Q_LORA, MLA_HEADS * LANES)], axis=1).astype(BF16)

    wukv = w_ukv[0].reshape(KV_LORA, MLA_HEADS, MLA_NOPE + MLA_V)
    w_ukt = jnp.transpose(wukv[..., :MLA_NOPE], (1, 2, 0)).astype(BF16)
    w_uv = jnp.transpose(wukv[..., MLA_NOPE:], (1, 0, 2)).astype(BF16)

    r = lambda a: a.reshape(1, -1).astype(F32)
    return dict(norm_mix=r(norm_mix[0]), w_in=w_in_p, w_gk=w_gk_p, b_gk=r(b_gk[0]), q_norm=r(q_norm[0]),
                kv_norm=r(kv_norm[0]), w_uq=w_uq_p, w_ukt=w_ukt, w_uv=w_uv, gla_norm=r(gla_norm[0]),
                w_out=w_out[0].astype(BF16), norm_ffn=r(norm_ffn[0]), w_gate=w_gate[0].astype(BF16),
                w_up=w_up[0].astype(BF16), w_down=w_down[0].astype(BF16), norm_final=r(norm_final))


def _rope_tables(pos):
    half = MLA_ROPE // 2
    inv = ROPE_THETA ** (-jnp.arange(half, dtype=F32) / half)
    ang = pos.astype(F32)[:, None] * inv[None, :]
    cos, sin = jnp.cos(ang), jnp.sin(ang)
    z = jnp.zeros((pos.shape[0], LANES - MLA_ROPE), F32)
    return jnp.concatenate([cos, cos, z], axis=1), jnp.concatenate([-sin, sin, z], axis=1)


def kernel(x_prompt, x_sample, cache_ckv, cache_kpe, state_gla, page_table, meta_tokens, norm_mix, w_in, w_gk, b_gk, gla_norm, q_norm, kv_norm, w_uq, w_ukv, w_out, norm_ffn, w_gate, w_up, w_down, norm_final):
    n_b, seq, d = x_prompt.shape
    n_dec, t_dec, _ = x_sample.shape
    assert w_in.shape[0] == 1 and t_dec == 1 and d == D_MODEL
    n_pages = page_table.shape[1]
    past = n_pages * PAGE_SIZE
    hk = GLA_HEADS * GLA_DK
    w = _prep_weights(norm_mix, w_in, w_gk, b_gk, gla_norm, q_norm, kv_norm, w_uq, w_ukv, w_out,
                      norm_ffn, w_gate, w_up, w_down, norm_final)

    small_rows = 2 * LANES
    assert N_META <= LANES and n_dec == LANES
    xs = x_sample[:, 0]
    x_small = jnp.concatenate([meta_tokens.astype(F32), jnp.zeros((LANES - N_META, d), F32), xs], axis=0)
    pos_small = jnp.concatenate([jnp.arange(N_META), jnp.zeros((LANES - N_META,), jnp.int32),
                                 jnp.full((n_dec,), past, jnp.int32)])
    sm = _project(x_small, *_rope_tables(pos_small), w, small_rows)
    xp = x_prompt.reshape(n_b * seq, d)
    pr = _project(xp, *_rope_tables(N_META + jnp.arange(seq)), w, 256)

    gla_in = ("gq", "gk", "lg", "gv", "g")
    _, s_meta = _gla_scan(*(sm[n][:N_META] for n in gla_in), w["gla_norm"],
                          jnp.zeros((1, hk, GLA_DV), F32), 1, GLA_STEP)
    og_p, s_prompt = _gla_scan(*(pr[n] for n in gla_in), w["gla_norm"], s_meta, n_b, 128)
    og_s, s_sample = _gla_token(*(sm[n][LANES:] for n in gla_in), w["gla_norm"],
                                state_gla[0].reshape(n_dec, hk, GLA_DV))

    kmeta = jnp.pad(sm["kcat"][:N_META], ((0, LANES - N_META), (0, 0)))
    olat_p = _mla_prompt(pr["qcat"], pr["kcat"], kmeta, n_b, 256)
    qd = jnp.pad(jnp.transpose(sm["qcat"][:, LANES:], (1, 0, 2)), ((0, 0), (0, Q_ROWS - MLA_HEADS), (0, 0)))
    qt = jnp.pad(jnp.transpose(sm["qcat"][:, LANES:, :KV_LORA], (1, 2, 0)),
                 ((0, 0), (0, 0), (0, LANES - MLA_HEADS)))
    cache_kpe_t = jnp.swapaxes(cache_kpe, 2, 3).reshape(-1, MLA_ROPE, PAGE_SIZE)
    o_dec = _mla_decode(page_table, qd, qt, sm["kcat"][LANES:, None, :], sm["ckv"][LANES:, None, :],
                        cache_ckv.reshape(-1, PAGE_SIZE, KV_LORA), cache_kpe_t, 32)
    olat_s = jnp.transpose(o_dec[:, :MLA_HEADS], (1, 0, 2)).astype(BF16)

    y_prompt = _ffn(xp, og_p, olat_p, w, 256).reshape(n_b, seq, d)
    y_sample = _ffn(xs, og_s, olat_s, w, LANES).reshape(n_dec, 1, d)

    bcast = lambda a: jnp.broadcast_to(a[None, :N_META], (n_b, N_META, a.shape[-1]))
    ckv_prompt = jnp.concatenate([bcast(sm["ckv"]), pr["ckv"].reshape(n_b, seq, KV_LORA)], axis=1)[None]
    kpe_prompt = jnp.concatenate([bcast(sm["kpe"]), pr["kpe"].reshape(n_b, seq, MLA_ROPE)], axis=1)[None]
    gla_prompt = s_prompt.reshape(1, n_b, GLA_HEADS, GLA_DK, GLA_DV)
    ckv_sample = sm["ckv"][LANES:].reshape(1, n_dec, 1, KV_LORA)
    kpe_sample = sm["kpe"][LANES:].reshape(1, n_dec, 1, MLA_ROPE)
    gla_sample = s_sample.reshape(1, n_dec, GLA_HEADS, GLA_DK, GLA_DV)
    return (y_prompt, y_sample, ckv_prompt, kpe_prompt, gla_prompt, ckv_sample, kpe_sample, gla_sample)
```

```python
import functools

import jax
import jax.numpy as jnp
from jax import lax
from jax.experimental import pallas as pl
from jax.experimental.pallas import tpu as pltpu

F32 = jnp.float32
BF16 = jnp.bfloat16

D_MODEL = 1024
N_META = 16
EPS = 1e-6
GLA_HEADS = 4
GLA_DV = 128
GLA_DK = 64
GLA_GATE_RANK = 16
GLA_GATE_NORM = 16.0
MLA_HEADS = 4
MLA_NOPE = 128
MLA_ROPE = 64
MLA_V = 128
Q_LORA = 384
KV_LORA = 256
ROPE_THETA = 10000.0
MLA_SCALE = (MLA_NOPE + MLA_ROPE) ** -0.5
PAGE_SIZE = 128
D_FF = 2816

LANES = 128
GLA_STEP = 16
GLA_CHUNK = 64
GLA_SAFE_LOG_DECAY = 40.0
QK_WIDTH = KV_LORA + LANES
NEG_BIG = -1e30
Q_PRESCALE = MLA_SCALE * 1.4426950408889634

_OFF_Q, _OFF_K, _OFF_V, _OFF_G, _OFF_CQ, _OFF_CKV, _OFF_TAIL, _IN_PACKED = (
    0, 256, 512, 1024, 1536, 1920, 2176, 2304)
_TAIL_GR = MLA_ROPE


def _rms(x, w):
    return x * lax.rsqrt(jnp.mean(x * x, axis=-1, keepdims=True) + EPS) * w


def _rope_tile(x, c, s):
    lane = lax.broadcasted_iota(jnp.int32, x.shape, 1)
    from_right = pltpu.roll(x, LANES - MLA_ROPE // 2, 1)
    from_left = pltpu.roll(x, MLA_ROPE // 2, 1)
    swapped = jnp.where(lane < MLA_ROPE // 2, from_right, from_left)
    return x * c + swapped * s


def _dot(a, b):
    return jnp.dot(a, b, preferred_element_type=F32)


def _dot_nt(a, b):
    return lax.dot_general(a, b, (((1,), (1,)), ((), ())), preferred_element_type=F32)


def _proj_kernel(x_ref, cos_ref, sin_ref, nmix_ref, win_ref, wgk_ref, bgk_ref, qn_ref, kvn_ref,
                 wuq_ref, wukt_ref,
                 gq_ref, gk_ref, gv_ref, lg_ref, g_ref, qcat_ref, kcat_ref, ckv_ref, kpe_ref):
    n = _rms(x_ref[...], nmix_ref[...]).astype(BF16)
    proj = _dot(n, win_ref[...])
    gq_ref[...] = proj[:, _OFF_Q:_OFF_K] * (GLA_DK ** -0.5)
    gk_ref[...] = proj[:, _OFF_K:_OFF_V]
    gv_ref[...] = proj[:, _OFF_V:_OFF_G]
    g_ref[...] = proj[:, _OFF_G:_OFF_CQ]
    tail = proj[:, _OFF_TAIL:_IN_PACKED]
    z = _dot(tail.astype(BF16), wgk_ref[...]) + bgk_ref[...]
    lg_ref[...] = jax.nn.log_sigmoid(z) * (1.0 / GLA_GATE_NORM)

    cos = cos_ref[...]
    sin = sin_ref[...]
    cq = _rms(proj[:, _OFF_CQ:_OFF_CKV], qn_ref[...]).astype(BF16)
    qh = _dot(cq, wuq_ref[...])
    for h in range(MLA_HEADS):
        q_nope = qh[:, h * MLA_NOPE:(h + 1) * MLA_NOPE].astype(BF16)
        q_lat = _dot(q_nope, wukt_ref[h])
        base = MLA_HEADS * MLA_NOPE + h * LANES
        q_pe = _rope_tile(qh[:, base:base + LANES], cos, sin)
        qcat_ref[h] = (jnp.concatenate([q_lat, q_pe], axis=-1) * Q_PRESCALE).astype(BF16)

    ckv = _rms(proj[:, _OFF_CKV:_OFF_TAIL], kvn_ref[...])
    ckv_ref[...] = ckv
    k_pe = _rope_tile(tail, cos, sin)
    kpe_ref[...] = k_pe[:, :MLA_ROPE]
    kcat_ref[...] = jnp.concatenate([ckv, k_pe], axis=-1).astype(BF16)


def _full_spec(a):
    nd = a.ndim
    return pl.BlockSpec(a.shape, lambda *_: (0,) * nd)


def _project(x, cos, sin, w, tm):
    t = x.shape[0]
    n_pos_blocks = cos.shape[0] // tm
    row = lambda width: pl.BlockSpec((tm, width), lambda i: (i, 0))
    pos = pl.BlockSpec((tm, LANES), lambda i: (i % n_pos_blocks, 0))
    weights = (w["norm_mix"], w["w_in"], w["w_gk"], w["b_gk"], w["q_norm"], w["kv_norm"],
               w["w_uq"], w["w_ukt"])
    out_shape = (
        jax.ShapeDtypeStruct((t, 256), F32), jax.ShapeDtypeStruct((t, 256), F32),
        jax.ShapeDtypeStruct((t, 512), F32), jax.ShapeDtypeStruct((t, 256), F32),
        jax.ShapeDtypeStruct((t, 512), F32),
        jax.ShapeDtypeStruct((MLA_HEADS, t, QK_WIDTH), BF16),
        jax.ShapeDtypeStruct((t, QK_WIDTH), BF16),
        jax.ShapeDtypeStruct((t, KV_LORA), F32), jax.ShapeDtypeStruct((t, MLA_ROPE), F32))
    out_specs = (row(256), row(256), row(512), row(256), row(512),
                 pl.BlockSpec((MLA_HEADS, tm, QK_WIDTH), lambda i: (0, i, 0)),
                 row(QK_WIDTH), row(KV_LORA), row(MLA_ROPE))
    names = ("gq", "gk", "gv", "lg", "g", "qcat", "kcat", "ckv", "kpe")
    outs = pl.pallas_call(
        _proj_kernel,
        grid=(t // tm,),
        in_specs=[row(D_MODEL), pos, pos] + [_full_spec(a) for a in weights],
        out_specs=out_specs,
        out_shape=out_shape,
        compiler_params=pltpu.CompilerParams(dimension_semantics=("parallel",)),
        name="project",
    )(x, cos, sin, *weights)
    return dict(zip(names, outs))


def _cum_log_decay(lg):
    n = lg.shape[0]
    tri = (lax.broadcasted_iota(jnp.int32, (n, n), 0) >= lax.broadcasted_iota(jnp.int32, (n, n), 1)).astype(F32)
    return jnp.dot(tri, lg, precision=lax.Precision.HIGHEST, preferred_element_type=F32)


def _head_masked_rows(x, pad_rows=0):
    lane_head = lax.broadcasted_iota(jnp.int32, x.shape, 1) // GLA_DK
    blocks = [jnp.where(lane_head == h, x, 0.0) for h in range(GLA_HEADS)]
    if pad_rows:
        blocks.append(jnp.zeros((pad_rows, x.shape[1]), x.dtype))
    return jnp.concatenate(blocks, axis=0)


def _head_stacked_values(v, pad_rows=0):
    blocks = [v[:, h * GLA_DV:(h + 1) * GLA_DV] for h in range(GLA_HEADS)]
    if pad_rows:
        blocks.append(jnp.zeros((pad_rows, GLA_DV), v.dtype))
    return jnp.concatenate(blocks, axis=0)


def _gla_state_update(state, kt, v, b_last, pad_rows=0):
    kbd = _head_masked_rows(kt, pad_rows)
    upd = _dot(kbd.T.astype(BF16), _head_stacked_values(v, pad_rows).astype(BF16))
    dec = jnp.broadcast_to(jnp.exp(b_last), (LANES, kt.shape[1])).T
    return dec * state + upd


def _gla_gate_out(o_heads, g, gn):
    normed = jnp.concatenate([_rms(o, gn) for o in o_heads], axis=-1)
    return (normed * (g * jax.nn.sigmoid(g))).astype(BF16)


def _gla_exact_step(state, q, k, lg, v, n_src=None):
    c = q.shape[0]
    n_src = c if n_src is None else n_src
    half = lax.broadcasted_iota(jnp.int32, (c, LANES), 1) // GLA_DK
    row_t = lax.broadcasted_iota(jnp.int32, q.shape, 0)
    b = _cum_log_decay(lg)
    b_last = b[c - 1:c, :]

    o_st = _dot(_head_masked_rows(q * jnp.exp(b)).astype(BF16), state.astype(BF16))
    o_heads = [o_st[h * c:(h + 1) * c] for h in range(GLA_HEADS)]

    for s in range(n_src):
        decay = jnp.exp(jnp.minimum(b - b[s:s + 1, :], 0.0))
        w = jnp.where(row_t >= s, q * k[s:s + 1, :] * decay, 0.0)
        for h in range(GLA_HEADS):
            tile = w[:, (h // 2) * LANES:(h // 2 + 1) * LANES]
            a = jnp.sum(jnp.where(half == h % 2, tile, 0.0), axis=1, keepdims=True)
            o_heads[h] = o_heads[h] + a * v[s:s + 1, h * GLA_DV:(h + 1) * GLA_DV]

    state = _gla_state_update(state, k * jnp.exp(b_last - b), v, b_last, LANES - GLA_HEADS * c)
    return state, o_heads


def _gla_matmul_chunk(state, q, k, b, v):
    c = GLA_CHUNK
    b_last = b[c - 1:c, :]
    qm = _head_masked_rows(q * jnp.exp(b)).astype(BF16)
    k_inv = (k * jnp.exp(-b)).astype(BF16)
    a = _dot_nt(qm, k_inv)
    row_t = lax.broadcasted_iota(jnp.int32, a.shape, 0) % c
    col_s = lax.broadcasted_iota(jnp.int32, a.shape, 1)
    a = jnp.where(col_s <= row_t, a, 0.0).astype(BF16)
    o_st = _dot(qm, state.astype(BF16))
    vb = v.astype(BF16)
    o_heads = [o_st[h * c:(h + 1) * c] + _dot(a[h * c:(h + 1) * c], vb[:, h * GLA_DV:(h + 1) * GLA_DV])
               for h in range(GLA_HEADS)]
    state = _gla_state_update(state, k * jnp.exp(b_last - b), v, b_last)
    return state, o_heads


def _gla_kernel(q_ref, k_ref, lg_ref, v_ref, g_ref, gn_ref, s0_ref, og_ref, sout_ref, s_scr, *, tt):
    j = pl.program_id(1)

    @pl.when(j == 0)
    def _():
        s_scr[...] = s0_ref[0]

    gn = gn_ref[...]

    def exact_steps(start, n_steps):
        state = s_scr[...]
        for u in range(n_steps):
            sl = slice(start + u * GLA_STEP, start + (u + 1) * GLA_STEP)
            state, o_heads = _gla_exact_step(state, q_ref[sl, :], k_ref[sl, :], lg_ref[sl, :], v_ref[sl, :])
            og_ref[sl, :] = _gla_gate_out(o_heads, g_ref[sl, :], gn)
        s_scr[...] = state

    if tt % GLA_CHUNK == 0:
        chunks = [slice(ci * GLA_CHUNK, (ci + 1) * GLA_CHUNK) for ci in range(tt // GLA_CHUNK)]
        bs = [_cum_log_decay(lg_ref[sl, :]) for sl in chunks]
        worst = bs[0][GLA_CHUNK - 1:GLA_CHUNK, :]
        for b in bs[1:]:
            worst = jnp.minimum(worst, b[GLA_CHUNK - 1:GLA_CHUNK, :])
        mild = jnp.max(-worst) <= GLA_SAFE_LOG_DECAY

        @pl.when(mild)
        def _():
            state = s_scr[...]
            for sl, b in zip(chunks, bs):
                state, o_heads = _gla_matmul_chunk(state, q_ref[sl, :], k_ref[sl, :], b, v_ref[sl, :])
                og_ref[sl, :] = _gla_gate_out(o_heads, g_ref[sl, :], gn)
            s_scr[...] = state

        @pl.when(jnp.logical_not(mild))
        def _():
            exact_steps(0, tt // GLA_STEP)
    else:
        exact_steps(0, tt // GLA_STEP)

    @pl.when(j == pl.num_programs(1) - 1)
    def _():
        sout_ref[0] = s_scr[...]


def _gla_scan(q, k, lg, v, g, gla_norm, s0, n_seq, tt):
    t = q.shape[0]
    n_tiles = t // n_seq // tt
    row = lambda width: pl.BlockSpec((tt, width), lambda b, j: (b * n_tiles + j, 0))
    s0_map = (lambda b, j: (b, 0, 0)) if s0.shape[0] == n_seq else (lambda b, j: (0, 0, 0))
    hk = GLA_HEADS * GLA_DK
    og, s_out = pl.pallas_call(
        functools.partial(_gla_kernel, tt=tt),
        grid=(n_seq, n_tiles),
        in_specs=[row(256), row(256), row(256), row(512), row(512), _full_spec(gla_norm),
                  pl.BlockSpec((1, hk, GLA_DV), s0_map)],
        out_specs=(row(512), pl.BlockSpec((1, hk, GLA_DV), lambda b, j: (b, 0, 0))),
        out_shape=(jax.ShapeDtypeStruct((t, 512), BF16),
                   jax.ShapeDtypeStruct((n_seq, hk, GLA_DV), F32)),
        scratch_shapes=[pltpu.VMEM((hk, GLA_DV), F32)],
        compiler_params=pltpu.CompilerParams(dimension_semantics=("parallel", "arbitrary")),
        name="gla_scan",
    )(q, k, lg, v, g, gla_norm, s0)
    return og, s_out


TOKEN_ROWS = 8
TOKEN_SEQS = 8


def _gla_token_kernel(q_ref, k_ref, lg_ref, v_ref, g_ref, gn_ref, s0_ref, og_ref, sout_ref):
    gn = gn_ref[...]
    for i in range(TOKEN_SEQS):
        sl = slice(i * TOKEN_ROWS, (i + 1) * TOKEN_ROWS)
        state, o_heads = _gla_exact_step(s0_ref[i], q_ref[sl, :], k_ref[sl, :], lg_ref[sl, :], v_ref[sl, :],
                                         n_src=1)
        og_ref[sl, :] = _gla_gate_out(o_heads, g_ref[sl, :], gn)
        sout_ref[i] = state


def _gla_token(q, k, lg, v, g, gla_norm, s0):
    n = q.shape[0]
    hk = GLA_HEADS * GLA_DK
    pad = lambda a: jnp.pad(a[:, None, :], ((0, 0), (0, TOKEN_ROWS - 1), (0, 0))).reshape(n * TOKEN_ROWS, -1)
    rows = TOKEN_SEQS * TOKEN_ROWS
    row = lambda width: pl.BlockSpec((rows, width), lambda i: (i, 0))
    st = pl.BlockSpec((TOKEN_SEQS, hk, GLA_DV), lambda i: (i, 0, 0))
    og, s_out = pl.pallas_call(
        _gla_token_kernel,
        grid=(n // TOKEN_SEQS,),
        in_specs=[row(256), row(256), row(256), row(512), row(512), _full_spec(gla_norm), st],
        out_specs=(row(512), st),
        out_shape=(jax.ShapeDtypeStruct((n * TOKEN_ROWS, 512), BF16),
                   jax.ShapeDtypeStruct((n, hk, GLA_DV), F32)),
        compiler_params=pltpu.CompilerParams(dimension_semantics=("parallel",)),
        name="gla_token",
    )(pad(q), pad(k), pad(lg), pad(v), pad(g), gla_norm, s0)
    return og[::TOKEN_ROWS], s_out


MLA_ROW_BLOCKS = 1


def _mla_prompt_kernel(q_ref, k_ref, km_ref, o_ref, m_scr, l_scr, acc_scr, *, tq):
    qi = pl.program_id(1)

    wide = lambda a, n: jnp.concatenate([a] * n, axis=-1)

    blk_heads = MLA_HEADS // MLA_ROW_BLOCKS
    blk_rows = blk_heads * tq
    blocks = [(slice(i * blk_heads, (i + 1) * blk_heads), slice(i * blk_rows, (i + 1) * blk_rows))
              for i in range(MLA_ROW_BLOCKS)]
    q_blk = lambda hs: q_ref[hs].reshape(blk_rows, QK_WIDTH)

    km = km_ref[...]
    for hs, rs in blocks:
        s = _dot_nt(q_blk(hs), km)
        col = lax.broadcasted_iota(jnp.int32, s.shape, 1)
        s = jnp.where(col < N_META, s, NEG_BIG)
        m0 = jnp.broadcast_to(jnp.max(s, axis=-1, keepdims=True), (blk_rows, LANES))
        p = jnp.exp2(s - m0)
        m_scr[rs, :] = m0
        l_scr[rs, :] = jnp.broadcast_to(jnp.sum(p, axis=-1, keepdims=True), (blk_rows, LANES))
        acc_scr[rs, :] = _dot(p.astype(BF16), km[:, :KV_LORA])

    def tile(kt, masked):
        k = k_ref[pl.ds(pl.multiple_of(kt * tq, tq), tq), :]
        for hs, rs in blocks:
            s = _dot_nt(q_blk(hs), k)
            if masked:
                row = lax.broadcasted_iota(jnp.int32, s.shape, 0) % tq
                col = lax.broadcasted_iota(jnp.int32, s.shape, 1)
                s = jnp.where(col <= row, s, NEG_BIG)
            m_prev = m_scr[rs, :]
            m_new = jnp.maximum(m_prev, jnp.max(s, axis=-1, keepdims=True))
            corr = jnp.exp2(m_prev - m_new)
            p = jnp.exp2(s - wide(m_new, tq // LANES))
            l_scr[rs, :] = l_scr[rs, :] * corr + jnp.sum(p, axis=-1, keepdims=True)
            acc_scr[rs, :] = (acc_scr[rs, :] * wide(corr, KV_LORA // LANES)
                              + _dot(p.astype(BF16), k[:, :KV_LORA]))
            m_scr[rs, :] = m_new

    def body(kt, carry):
        tile(kt, False)
        return carry

    lax.fori_loop(0, qi, body, 0)
    tile(qi, True)
    for hs, rs in blocks:
        o = acc_scr[rs, :] * wide(1.0 / l_scr[rs, :], KV_LORA // LANES)
        o_ref[hs] = o.reshape(blk_heads, tq, KV_LORA).astype(BF16)


def _mla_prompt(qcat, kcat, kmeta, n_seq, tq):
    t = kcat.shape[0]
    seq = t // n_seq
    nq = seq // tq
    return pl.pallas_call(
        functools.partial(_mla_prompt_kernel, tq=tq),
        grid=(n_seq, nq),
        in_specs=[pl.BlockSpec((MLA_HEADS, tq, QK_WIDTH), lambda b, i: (0, b * nq + i, 0)),
                  pl.BlockSpec((seq, QK_WIDTH), lambda b, i: (b, 0)),
                  _full_spec(kmeta)],
        out_specs=pl.BlockSpec((MLA_HEADS, tq, KV_LORA), lambda b, i: (0, b * nq + i, 0)),
        out_shape=jax.ShapeDtypeStruct((MLA_HEADS, t, KV_LORA), BF16),
        scratch_shapes=[pltpu.VMEM((MLA_HEADS * tq, LANES), F32), pltpu.VMEM((MLA_HEADS * tq, LANES), F32),
                        pltpu.VMEM((MLA_HEADS * tq, KV_LORA), F32)],
        compiler_params=pltpu.CompilerParams(dimension_semantics=("parallel", "arbitrary")),
        name="mla_prompt",
    )(qcat, kcat, kmeta)


Q_ROWS = 8
DEC_GROUP = 8


def _mla_decode_kernel(pt_ref, q_ref, kn_ref, cn_ref, ckv_hbm, kpe_hbm, o_ref,
                       ckv_buf, kpe_buf, sem, *, pg, nc, n_seq):
    s_id = pl.program_id(0)

    def start_chunk(seq_i, ci):
        for i in range(pg):
            page = pt_ref[seq_i, ci * pg + i]
            pltpu.make_async_copy(ckv_hbm.at[page], ckv_buf.at[ci, i], sem.at[0, ci]).start()
            pltpu.make_async_copy(kpe_hbm.at[page], kpe_buf.at[ci, i], sem.at[1, ci]).start()

    def wait_chunk(ci):
        pltpu.make_async_copy(ckv_hbm.at[pl.ds(0, pg)], ckv_buf.at[ci], sem.at[0, ci]).wait()
        pltpu.make_async_copy(kpe_hbm.at[pl.ds(0, pg)], kpe_buf.at[ci], sem.at[1, ci]).wait()

    @pl.when(s_id == 0)
    def _():
        for ci in range(nc):
            start_chunk(0, ci)

    nxt = jnp.minimum(s_id + 1, n_seq - 1)

    q = q_ref[0]
    q_pe = q[:, KV_LORA:KV_LORA + MLA_ROPE]
    q_rows = jnp.concatenate([q[:, :KV_LORA].astype(F32), jnp.zeros((LANES - Q_ROWS, KV_LORA), F32)], axis=0)
    q_t = q_rows.T.astype(BF16)
    group_keys = DEC_GROUP * PAGE_SIZE
    n_groups = pg // DEC_GROUP
    m = jnp.full((Q_ROWS, 1), NEG_BIG, F32)
    l = jnp.zeros((Q_ROWS, 1), F32)
    acc = jnp.zeros((Q_ROWS, KV_LORA), F32)
    for ci in range(nc):
        wait_chunk(ci)
        ckv_groups, parts = [], []
        for g in range(n_groups):
            ckv_g = ckv_buf[ci, g * DEC_GROUP:(g + 1) * DEC_GROUP].reshape(group_keys, KV_LORA).astype(BF16)
            ckv_groups.append(ckv_g)
            s_lat_t = _dot(ckv_g, q_t)
            for i in range(DEC_GROUP):
                s_lat = s_lat_t[i * PAGE_SIZE:(i + 1) * PAGE_SIZE].T[:Q_ROWS]
                kpe_t = kpe_buf[ci, g * DEC_GROUP + i].astype(BF16)
                parts.append(s_lat + _dot(q_pe, kpe_t))
        s = jnp.concatenate(parts, axis=-1)
        m_new = jnp.maximum(m, jnp.max(s, axis=-1, keepdims=True))
        corr = jnp.exp2(m - m_new)
        pb = jnp.exp2(s - m_new)
        l = l * corr + jnp.sum(pb, axis=-1, keepdims=True)
        pb = pb.astype(BF16)
        pv = _dot(pb[:, :group_keys], ckv_groups[0])
        for g in range(1, n_groups):
            pv = pv + _dot(pb[:, g * group_keys:(g + 1) * group_keys], ckv_groups[g])
        acc = acc * corr + pv
        m = m_new
        start_chunk(nxt, ci)

    s_self = jnp.sum(q.astype(F32) * kn_ref[0].astype(F32), axis=-1, keepdims=True)
    m_fin = jnp.maximum(m, s_self)
    corr = jnp.exp2(m - m_fin)
    p_self = jnp.exp2(s_self - m_fin)
    l_fin = l * corr + p_self
    o_ref[0] = (acc * corr + p_self * cn_ref[0]) * (1.0 / l_fin)

    @pl.when(s_id == n_seq - 1)
    def _():
        for ci in range(nc):
            wait_chunk(ci)


def _mla_decode(page_table, qd, kn, cn, cache_ckv, cache_kpe_t, pg):
    n_seq, n_pages = page_table.shape
    nc = n_pages // pg
    grid_spec = pltpu.PrefetchScalarGridSpec(
        num_scalar_prefetch=1,
        grid=(n_seq,),
        in_specs=[pl.BlockSpec((1, Q_ROWS, QK_WIDTH), lambda s, pt: (s, 0, 0)),
                  pl.BlockSpec((1, 1, QK_WIDTH), lambda s, pt: (s, 0, 0)),
                  pl.BlockSpec((1, 1, KV_LORA), lambda s, pt: (s, 0, 0)),
                  pl.BlockSpec(memory_space=pl.ANY),
                  pl.BlockSpec(memory_space=pl.ANY)],
        out_specs=pl.BlockSpec((1, Q_ROWS, KV_LORA), lambda s, pt: (s, 0, 0)),
        scratch_shapes=[pltpu.VMEM((nc, pg, PAGE_SIZE, KV_LORA), F32),
                        pltpu.VMEM((nc, pg, MLA_ROPE, PAGE_SIZE), F32),
                        pltpu.SemaphoreType.DMA((2, nc))])
    return pl.pallas_call(
        functools.partial(_mla_decode_kernel, pg=pg, nc=nc, n_seq=n_seq),
        grid_spec=grid_spec,
        out_shape=jax.ShapeDtypeStruct((n_seq, Q_ROWS, KV_LORA), F32),
        compiler_params=pltpu.CompilerParams(dimension_semantics=("arbitrary",)),
        name="mla_decode",
    )(page_table, qd, kn, cn, cache_ckv, cache_kpe_t)


def _ffn_kernel(x_ref, og_ref, ol_ref, wuv_ref, wout_ref, nffn_ref, wg_ref, wu_ref, wd_ref, nfin_ref, y_ref):
    om = [_dot(ol_ref[h], wuv_ref[h]).astype(BF16) for h in range(MLA_HEADS)]
    cat = jnp.concatenate([og_ref[...]] + om, axis=-1)
    h1 = x_ref[...] + _dot(cat, wout_ref[...])
    n = _rms(h1, nffn_ref[...]).astype(BF16)
    gate = _dot(n, wg_ref[...])
    up = _dot(n, wu_ref[...])
    act = (gate * jax.nn.sigmoid(gate) * up).astype(BF16)
    h2 = h1 + _dot(act, wd_ref[...])
    y_ref[...] = _rms(h2, nfin_ref[...])


def _ffn(x, og, olat, w, tm):
    t = x.shape[0]
    row = lambda width: pl.BlockSpec((tm, width), lambda i: (i, 0))
    weights = (w["w_uv"], w["w_out"], w["norm_ffn"], w["w_gate"], w["w_up"], w["w_down"], w["norm_final"])
    return pl.pallas_call(
        _ffn_kernel,
        grid=(t // tm,),
        in_specs=[row(D_MODEL), row(512), pl.BlockSpec((MLA_HEADS, tm, KV_LORA), lambda i: (0, i, 0))]
                 + [_full_spec(a) for a in weights],
        out_specs=row(D_MODEL),
        out_shape=jax.ShapeDtypeStruct((t, D_MODEL), F32),
        compiler_params=pltpu.CompilerParams(dimension_semantics=("parallel",)),
        name="ffn",
    )(x, og, olat, *weights)


def _prep_weights(norm_mix, w_in, w_gk, b_gk, gla_norm, q_norm, kv_norm, w_uq, w_ukv, w_out,
                  norm_ffn, w_gate, w_up, w_down, norm_final):
    hk = GLA_HEADS * GLA_DK
    hv = GLA_HEADS * GLA_DV
    sizes = (hk, hk, hv, GLA_GATE_RANK, hv, Q_LORA, KV_LORA, MLA_ROPE)
    bounds = [0]
    for sz in sizes:
        bounds.append(bounds[-1] + sz)
    wq, wk, wv, wgr, wg, wcq, wckv, wkpe = (w_in[0][:, bounds[i]:bounds[i + 1]] for i in range(8))
    pad = jnp.zeros((D_MODEL, _IN_PACKED - _OFF_TAIL - MLA_ROPE - GLA_GATE_RANK), w_in.dtype)
    w_in_p = jnp.concatenate([wq, wk, wv, wg, wcq, wckv, wkpe, wgr, pad], axis=1).astype(BF16)

    w_gk_p = jnp.zeros((LANES, hk), F32).at[_TAIL_GR:_TAIL_GR + GLA_GATE_RANK].set(w_gk[0]).astype(BF16)

    wuq = w_uq[0].reshape(Q_LORA, MLA_HEADS, MLA_NOPE + MLA_ROPE)
    wuq_nope = wuq[..., :MLA_NOPE].reshape(Q_LORA, MLA_HEADS * MLA_NOPE)
    wuq_rope = jnp.pad(wuq[..., MLA_NOPE:], ((0, 0), (0, 0), (0, LANES - MLA_ROPE)))
    w_uq_p = jnp.concatenate([wuq_nope, wuq_rope.reshape(Q_LORA, MLA_HEADS * LANES)], axis=1).astype(BF16)

    wukv = w_ukv[0].reshape(KV_LORA, MLA_HEADS, MLA_NOPE + MLA_V)
    w_ukt = jnp.transpose(wukv[..., :MLA_NOPE], (1, 2, 0)).astype(BF16)
    w_uv = jnp.transpose(wukv[..., MLA_NOPE:], (1, 0, 2)).astype(BF16)

    r = lambda a: a.reshape(1, -1).astype(F32)
    return dict(norm_mix=r(norm_mix[0]), w_in=w_in_p, w_gk=w_gk_p, b_gk=r(b_gk[0]), q_norm=r(q_norm[0]),
                kv_norm=r(kv_norm[0]), w_uq=w_uq_p, w_ukt=w_ukt, w_uv=w_uv, gla_norm=r(gla_norm[0]),
                w_out=w_out[0].astype(BF16), norm_ffn=r(norm_ffn[0]), w_gate=w_gate[0].astype(BF16),
                w_up=w_up[0].astype(BF16), w_down=w_down[0].astype(BF16), norm_final=r(norm_final))


def _rope_tables(pos):
    half = MLA_ROPE // 2
    inv = ROPE_THETA ** (-jnp.arange(half, dtype=F32) / half)
    ang = pos.astype(F32)[:, None] * inv[None, :]
    cos, sin = jnp.cos(ang), jnp.sin(ang)
    z = jnp.zeros((pos.shape[0], LANES - MLA_ROPE), F32)
    return jnp.concatenate([cos, cos, z], axis=1), jnp.concatenate([-sin, sin, z], axis=1)


def kernel(x_prompt, x_sample, cache_ckv, cache_kpe, state_gla, page_table, meta_tokens, norm_mix, w_in, w_gk, b_gk, gla_norm, q_norm, kv_norm, w_uq, w_ukv, w_out, norm_ffn, w_gate, w_up, w_down, norm_final):
    n_b, seq, d = x_prompt.shape
    n_dec, t_dec, _ = x_sample.shape
    assert w_in.shape[0] == 1 and t_dec == 1 and d == D_MODEL
    n_pages = page_table.shape[1]
    past = n_pages * PAGE_SIZE
    hk = GLA_HEADS * GLA_DK
    w = _prep_weights(norm_mix, w_in, w_gk, b_gk, gla_norm, q_norm, kv_norm, w_uq, w_ukv, w_out,
                      norm_ffn, w_gate, w_up, w_down, norm_final)

    small_rows = 2 * LANES
    assert N_META <= LANES and n_dec == LANES
    xs = x_sample[:, 0]
    x_small = jnp.concatenate([meta_tokens.astype(F32), jnp.zeros((LANES - N_META, d), F32), xs], axis=0)
    pos_small = jnp.concatenate([jnp.arange(N_META), jnp.zeros((LANES - N_META,), jnp.int32),
                                 jnp.full((n_dec,), past, jnp.int32)])
    sm = _project(x_small, *_rope_tables(pos_small), w, small_rows)
    xp = x_prompt.reshape(n_b * seq, d)
    pr = _project(xp, *_rope_tables(N_META + jnp.arange(seq)), w, 512)

    gla_in = ("gq", "gk", "lg", "gv", "g")
    _, s_meta = _gla_scan(*(sm[n][:N_META] for n in gla_in), w["gla_norm"],
                          jnp.zeros((1, hk, GLA_DV), F32), 1, GLA_STEP)
    og_p, s_prompt = _gla_scan(*(pr[n] for n in gla_in), w["gla_norm"], s_meta, n_b, 256)
    og_s, s_sample = _gla_token(*(sm[n][LANES:] for n in gla_in), w["gla_norm"],
                                state_gla[0].reshape(n_dec, hk, GLA_DV))

    kmeta = jnp.pad(sm["kcat"][:N_META], ((0, LANES - N_META), (0, 0)))
    olat_p = _mla_prompt(pr["qcat"], pr["kcat"], kmeta, n_b, 512)
    qd = jnp.pad(jnp.transpose(sm["qcat"][:, LANES:], (1, 0, 2)), ((0, 0), (0, Q_ROWS - MLA_HEADS), (0, 0)))
    cache_kpe_t = jnp.swapaxes(cache_kpe, 2, 3).reshape(-1, MLA_ROPE, PAGE_SIZE)
    o_dec = _mla_decode(page_table, qd, sm["kcat"][LANES:, None, :], sm["ckv"][LANES:, None, :],
                        cache_ckv.reshape(-1, PAGE_SIZE, KV_LORA), cache_kpe_t, 32)
    olat_s = jnp.transpose(o_dec[:, :MLA_HEADS], (1, 0, 2)).astype(BF16)

    y_prompt = _ffn(xp, og_p, olat_p, w, 512).reshape(n_b, seq, d)
    y_sample = _ffn(xs, og_s, olat_s, w, LANES).reshape(n_dec, 1, d)

    bcast = lambda a: jnp.broadcast_to(a[None, :N_META], (n_b, N_META, a.shape[-1]))
    ckv_prompt = jnp.concatenate([bcast(sm["ckv"]), pr["ckv"].reshape(n_b, seq, KV_LORA)], axis=1)[None]
    kpe_prompt = jnp.concatenate([bcast(sm["kpe"]), pr["kpe"].reshape(n_b, seq, MLA_ROPE)], axis=1)[None]
    gla_prompt = s_prompt.reshape(1, n_b, GLA_HEADS, GLA_DK, GLA_DV)
    ckv_sample = sm["ckv"][LANES:].reshape(1, n_dec, 1, KV_LORA)
    kpe_sample = sm["kpe"][LANES:].reshape(1, n_dec, 1, MLA_ROPE)
    gla_sample = s_sample.reshape(1, n_dec, GLA_HEADS, GLA_DK, GLA_DV)
    return (y_prompt, y_sample, ckv_prompt, kpe_prompt, gla_prompt, ckv_sample, kpe_sample, gla_sample)
```

```python
import functools

import jax
import jax.numpy as jnp
from jax import lax
from jax.experimental import pallas as pl
from jax.experimental.pallas import tpu as pltpu

F32 = jnp.float32
BF16 = jnp.bfloat16

D_MODEL = 1024
N_META = 16
EPS = 1e-6
GLA_HEADS = 4
GLA_DV = 128
GLA_DK = 64
GLA_GATE_RANK = 16
GLA_GATE_NORM = 16.0
MLA_HEADS = 4
MLA_NOPE = 128
MLA_ROPE = 64
MLA_V = 128
Q_LORA = 384
KV_LORA = 256
ROPE_THETA = 10000.0
MLA_SCALE = (MLA_NOPE + MLA_ROPE) ** -0.5
PAGE_SIZE = 128
D_FF = 2816

LANES = 128
GLA_STEP = 16
GLA_CHUNK = 64
GLA_SAFE_LOG_DECAY = 40.0
QK_WIDTH = KV_LORA + LANES
NEG_BIG = -1e30
Q_PRESCALE = MLA_SCALE * 1.4426950408889634

_OFF_Q, _OFF_K, _OFF_V, _OFF_G, _OFF_CQ, _OFF_CKV, _OFF_TAIL, _IN_PACKED = (
    0, 256, 512, 1024, 1536, 1920, 2176, 2304)
_TAIL_GR = MLA_ROPE


def _rms(x, w):
    return x * lax.rsqrt(jnp.mean(x * x, axis=-1, keepdims=True) + EPS) * w


def _rope_tile(x, c, s):
    lane = lax.broadcasted_iota(jnp.int32, x.shape, 1)
    from_right = pltpu.roll(x, LANES - MLA_ROPE // 2, 1)
    from_left = pltpu.roll(x, MLA_ROPE // 2, 1)
    swapped = jnp.where(lane < MLA_ROPE // 2, from_right, from_left)
    return x * c + swapped * s


def _dot(a, b):
    return jnp.dot(a, b, preferred_element_type=F32)


def _dot_nt(a, b):
    return lax.dot_general(a, b, (((1,), (1,)), ((), ())), preferred_element_type=F32)


def _proj_kernel(x_ref, cos_ref, sin_ref, nmix_ref, win_ref, wgk_ref, bgk_ref, qn_ref, kvn_ref,
                 wuq_ref, wukt_ref,
                 gq_ref, gk_ref, gv_ref, lg_ref, g_ref, qcat_ref, kcat_ref, ckv_ref, kpe_ref):
    n = _rms(x_ref[...], nmix_ref[...]).astype(BF16)
    proj = _dot(n, win_ref[...])
    gq_ref[...] = proj[:, _OFF_Q:_OFF_K] * (GLA_DK ** -0.5)
    gk_ref[...] = proj[:, _OFF_K:_OFF_V]
    gv_ref[...] = proj[:, _OFF_V:_OFF_G]
    g_ref[...] = proj[:, _OFF_G:_OFF_CQ]
    tail = proj[:, _OFF_TAIL:_IN_PACKED]
    z = _dot(tail.astype(BF16), wgk_ref[...]) + bgk_ref[...]
    lg_ref[...] = jax.nn.log_sigmoid(z) * (1.0 / GLA_GATE_NORM)

    cos = cos_ref[...]
    sin = sin_ref[...]
    cq = _rms(proj[:, _OFF_CQ:_OFF_CKV], qn_ref[...]).astype(BF16)
    qh = _dot(cq, wuq_ref[...])
    for h in range(MLA_HEADS):
        q_nope = qh[:, h * MLA_NOPE:(h + 1) * MLA_NOPE].astype(BF16)
        q_lat = _dot(q_nope, wukt_ref[h])
        base = MLA_HEADS * MLA_NOPE + h * LANES
        q_pe = _rope_tile(qh[:, base:base + LANES], cos, sin)
        qcat_ref[h] = (jnp.concatenate([q_lat, q_pe], axis=-1) * Q_PRESCALE).astype(BF16)

    ckv = _rms(proj[:, _OFF_CKV:_OFF_TAIL], kvn_ref[...])
    ckv_ref[...] = ckv
    k_pe = _rope_tile(tail, cos, sin)
    kpe_ref[...] = k_pe[:, :MLA_ROPE]
    kcat_ref[...] = jnp.concatenate([ckv, k_pe], axis=-1).astype(BF16)


def _full_spec(a):
    nd = a.ndim
    return pl.BlockSpec(a.shape, lambda *_: (0,) * nd)


def _project(x, cos, sin, w, tm):
    t = x.shape[0]
    n_pos_blocks = cos.shape[0] // tm
    row = lambda width: pl.BlockSpec((tm, width), lambda i: (i, 0))
    pos = pl.BlockSpec((tm, LANES), lambda i: (i % n_pos_blocks, 0))
    weights = (w["norm_mix"], w["w_in"], w["w_gk"], w["b_gk"], w["q_norm"], w["kv_norm"],
               w["w_uq"], w["w_ukt"])
    out_shape = (
        jax.ShapeDtypeStruct((t, 256), F32), jax.ShapeDtypeStruct((t, 256), F32),
        jax.ShapeDtypeStruct((t, 512), F32), jax.ShapeDtypeStruct((t, 256), F32),
        jax.ShapeDtypeStruct((t, 512), F32),
        jax.ShapeDtypeStruct((MLA_HEADS, t, QK_WIDTH), BF16),
        jax.ShapeDtypeStruct((t, QK_WIDTH), BF16),
        jax.ShapeDtypeStruct((t, KV_LORA), F32), jax.ShapeDtypeStruct((t, MLA_ROPE), F32))
    out_specs = (row(256), row(256), row(512), row(256), row(512),
                 pl.BlockSpec((MLA_HEADS, tm, QK_WIDTH), lambda i: (0, i, 0)),
                 row(QK_WIDTH), row(KV_LORA), row(MLA_ROPE))
    names = ("gq", "gk", "gv", "lg", "g", "qcat", "kcat", "ckv", "kpe")
    outs = pl.pallas_call(
        _proj_kernel,
        grid=(t // tm,),
        in_specs=[row(D_MODEL), pos, pos] + [_full_spec(a) for a in weights],
        out_specs=out_specs,
        out_shape=out_shape,
        compiler_params=pltpu.CompilerParams(dimension_semantics=("parallel",)),
        name="project",
    )(x, cos, sin, *weights)
    return dict(zip(names, outs))


def _cum_log_decay(lg):
    n = lg.shape[0]
    tri = (lax.broadcasted_iota(jnp.int32, (n, n), 0) >= lax.broadcasted_iota(jnp.int32, (n, n), 1)).astype(F32)
    return jnp.dot(tri, lg, precision=lax.Precision.HIGHEST, preferred_element_type=F32)


def _head_masked_rows(x, pad_rows=0):
    lane_head = lax.broadcasted_iota(jnp.int32, x.shape, 1) // GLA_DK
    blocks = [jnp.where(lane_head == h, x, 0.0) for h in range(GLA_HEADS)]
    if pad_rows:
        blocks.append(jnp.zeros((pad_rows, x.shape[1]), x.dtype))
    return jnp.concatenate(blocks, axis=0)


def _head_stacked_values(v, pad_rows=0):
    blocks = [v[:, h * GLA_DV:(h + 1) * GLA_DV] for h in range(GLA_HEADS)]
    if pad_rows:
        blocks.append(jnp.zeros((pad_rows, GLA_DV), v.dtype))
    return jnp.concatenate(blocks, axis=0)


def _gla_state_update(state, kt, v, b_last, pad_rows=0):
    kbd = _head_masked_rows(kt, pad_rows)
    upd = _dot(kbd.T.astype(BF16), _head_stacked_values(v, pad_rows).astype(BF16))
    dec = jnp.broadcast_to(jnp.exp(b_last), (LANES, kt.shape[1])).T
    return dec * state + upd


def _gla_gate_out(o_heads, g, gn):
    normed = jnp.concatenate([_rms(o, gn) for o in o_heads], axis=-1)
    return (normed * (g * jax.nn.sigmoid(g))).astype(BF16)


def _gla_exact_step(state, q, k, lg, v, n_src=None):
    c = q.shape[0]
    n_src = c if n_src is None else n_src
    half = lax.broadcasted_iota(jnp.int32, (c, LANES), 1) // GLA_DK
    row_t = lax.broadcasted_iota(jnp.int32, q.shape, 0)
    b = _cum_log_decay(lg)
    b_last = b[c - 1:c, :]

    o_st = _dot(_head_masked_rows(q * jnp.exp(b)).astype(BF16), state.astype(BF16))
    o_heads = [o_st[h * c:(h + 1) * c] for h in range(GLA_HEADS)]

    for s in range(n_src):
        decay = jnp.exp(jnp.minimum(b - b[s:s + 1, :], 0.0))
        w = jnp.where(row_t >= s, q * k[s:s + 1, :] * decay, 0.0)
        for h in range(GLA_HEADS):
            tile = w[:, (h // 2) * LANES:(h // 2 + 1) * LANES]
            a = jnp.sum(jnp.where(half == h % 2, tile, 0.0), axis=1, keepdims=True)
            o_heads[h] = o_heads[h] + a * v[s:s + 1, h * GLA_DV:(h + 1) * GLA_DV]

    state = _gla_state_update(state, k * jnp.exp(b_last - b), v, b_last, LANES - GLA_HEADS * c)
    return state, o_heads


def _gla_matmul_chunk(state, q, k, b, v):
    c = GLA_CHUNK
    b_last = b[c - 1:c, :]
    qm = _head_masked_rows(q * jnp.exp(b)).astype(BF16)
    k_inv = (k * jnp.exp(-b)).astype(BF16)
    a = _dot_nt(qm, k_inv)
    row_t = lax.broadcasted_iota(jnp.int32, a.shape, 0) % c
    col_s = lax.broadcasted_iota(jnp.int32, a.shape, 1)
    a = jnp.where(col_s <= row_t, a, 0.0).astype(BF16)
    o_st = _dot(qm, state.astype(BF16))
    vb = v.astype(BF16)
    o_heads = [o_st[h * c:(h + 1) * c] + _dot(a[h * c:(h + 1) * c], vb[:, h * GLA_DV:(h + 1) * GLA_DV])
               for h in range(GLA_HEADS)]
    state = _gla_state_update(state, k * jnp.exp(b_last - b), v, b_last)
    return state, o_heads


def _gla_kernel(q_ref, k_ref, lg_ref, v_ref, g_ref, gn_ref, s0_ref, og_ref, sout_ref, s_scr, *, tt):
    j = pl.program_id(1)

    @pl.when(j == 0)
    def _():
        s_scr[...] = s0_ref[0]

    gn = gn_ref[...]

    def exact_steps(start, n_steps):
        state = s_scr[...]
        for u in range(n_steps):
            sl = slice(start + u * GLA_STEP, start + (u + 1) * GLA_STEP)
            state, o_heads = _gla_exact_step(state, q_ref[sl, :], k_ref[sl, :], lg_ref[sl, :], v_ref[sl, :])
            og_ref[sl, :] = _gla_gate_out(o_heads, g_ref[sl, :], gn)
        s_scr[...] = state

    if tt % GLA_CHUNK == 0:
        chunks = [slice(ci * GLA_CHUNK, (ci + 1) * GLA_CHUNK) for ci in range(tt // GLA_CHUNK)]
        bs = [_cum_log_decay(lg_ref[sl, :]) for sl in chunks]
        worst = bs[0][GLA_CHUNK - 1:GLA_CHUNK, :]
        for b in bs[1:]:
            worst = jnp.minimum(worst, b[GLA_CHUNK - 1:GLA_CHUNK, :])
        mild = jnp.max(-worst) <= GLA_SAFE_LOG_DECAY

        @pl.when(mild)
        def _():
            state = s_scr[...]
            for sl, b in zip(chunks, bs):
                state, o_heads = _gla_matmul_chunk(state, q_ref[sl, :], k_ref[sl, :], b, v_ref[sl, :])
                og_ref[sl, :] = _gla_gate_out(o_heads, g_ref[sl, :], gn)
            s_scr[...] = state

        @pl.when(jnp.logical_not(mild))
        def _():
            exact_steps(0, tt // GLA_STEP)
    else:
        exact_steps(0, tt // GLA_STEP)

    @pl.when(j == pl.num_programs(1) - 1)
    def _():
        sout_ref[0] = s_scr[...]


def _gla_scan(q, k, lg, v, g, gla_norm, s0, n_seq, tt):
    t = q.shape[0]
    n_tiles = t // n_seq // tt
    row = lambda width: pl.BlockSpec((tt, width), lambda b, j: (b * n_tiles + j, 0))
    s0_map = (lambda b, j: (b, 0, 0)) if s0.shape[0] == n_seq else (lambda b, j: (0, 0, 0))
    hk = GLA_HEADS * GLA_DK
    og, s_out = pl.pallas_call(
        functools.partial(_gla_kernel, tt=tt),
        grid=(n_seq, n_tiles),
        in_specs=[row(256), row(256), row(256), row(512), row(512), _full_spec(gla_norm),
                  pl.BlockSpec((1, hk, GLA_DV), s0_map)],
        out_specs=(row(512), pl.BlockSpec((1, hk, GLA_DV), lambda b, j: (b, 0, 0))),
        out_shape=(jax.ShapeDtypeStruct((t, 512), BF16),
                   jax.ShapeDtypeStruct((n_seq, hk, GLA_DV), F32)),
        scratch_shapes=[pltpu.VMEM((hk, GLA_DV), F32)],
        compiler_params=pltpu.CompilerParams(dimension_semantics=("parallel", "arbitrary")),
        name="gla_scan",
    )(q, k, lg, v, g, gla_norm, s0)
    return og, s_out


TOKEN_ROWS = 8
TOKEN_SEQS = 8


def _gla_token_kernel(q_ref, k_ref, lg_ref, v_ref, g_ref, gn_ref, s0_ref, og_ref, sout_ref):
    gn = gn_ref[...]
    for i in range(TOKEN_SEQS):
        sl = slice(i * TOKEN_ROWS, (i + 1) * TOKEN_ROWS)
        state, o_heads = _gla_exact_step(s0_ref[i], q_ref[sl, :], k_ref[sl, :], lg_ref[sl, :], v_ref[sl, :],
                                         n_src=1)
        og_ref[sl, :] = _gla_gate_out(o_heads, g_ref[sl, :], gn)
        sout_ref[i] = state


def _gla_token(q, k, lg, v, g, gla_norm, s0):
    n = q.shape[0]
    hk = GLA_HEADS * GLA_DK
    pad = lambda a: jnp.pad(a[:, None, :], ((0, 0), (0, TOKEN_ROWS - 1), (0, 0))).reshape(n * TOKEN_ROWS, -1)
    rows = TOKEN_SEQS * TOKEN_ROWS
    row = lambda width: pl.BlockSpec((rows, width), lambda i: (i, 0))
    st = pl.BlockSpec((TOKEN_SEQS, hk, GLA_DV), lambda i: (i, 0, 0))
    og, s_out = pl.pallas_call(
        _gla_token_kernel,
        grid=(n // TOKEN_SEQS,),
        in_specs=[row(256), row(256), row(256), row(512), row(512), _full_spec(gla_norm), st],
        out_specs=(row(512), st),
        out_shape=(jax.ShapeDtypeStruct((n * TOKEN_ROWS, 512), BF16),
                   jax.ShapeDtypeStruct((n, hk, GLA_DV), F32)),
        compiler_params=pltpu.CompilerParams(dimension_semantics=("parallel",)),
        name="gla_token",
    )(pad(q), pad(k), pad(lg), pad(v), pad(g), gla_norm, s0)
    return og[::TOKEN_ROWS], s_out


def _mla_prompt_kernel(q_ref, k_ref, km_ref, o_ref, s_scr, sm_scr, mx_scr, l_scr, acc_scr, *, tq):
    qi = pl.program_id(1)
    m_rows = MLA_HEADS * tq
    n_lane_tiles = tq // LANES
    q = q_ref[...].reshape(m_rows, QK_WIDTH)
    lane_fold = lambda a, op: functools.reduce(
        op, [a[:, i * LANES:(i + 1) * LANES] for i in range(a.shape[1] // LANES)])
    wide = lambda a, n: jnp.concatenate([a] * n, axis=-1)
    key_tile = lambda kt: k_ref[pl.ds(pl.multiple_of(kt * tq, tq), tq), :]

    km = km_ref[...]
    s = _dot_nt(q, km)
    col = lax.broadcasted_iota(jnp.int32, s.shape, 1)
    s = jnp.where(col < N_META, s, NEG_BIG)
    sm_scr[...] = s
    mx_scr[...] = s

    def score_tile(kt, masked):
        s = _dot_nt(q, key_tile(kt))
        if masked:
            row = lax.broadcasted_iota(jnp.int32, s.shape, 0) % tq
            col = lax.broadcasted_iota(jnp.int32, s.shape, 1)
            s = jnp.where(col <= row, s, NEG_BIG)
        s_scr[kt] = s
        mx_scr[...] = jnp.maximum(mx_scr[...], lane_fold(s, jnp.maximum))

    def score_body(kt, carry):
        score_tile(kt, False)
        return carry

    lax.fori_loop(0, qi, score_body, 0)
    score_tile(qi, True)
    m = jnp.broadcast_to(jnp.max(mx_scr[...], axis=-1, keepdims=True), (m_rows, LANES))

    p = jnp.exp2(sm_scr[...] - m)
    l_scr[...] = p
    acc_scr[...] = _dot(p.astype(BF16), km[:, :KV_LORA])
    m_wide = wide(m, n_lane_tiles)

    def value_body(kt, carry):
        p = jnp.exp2(s_scr[kt] - m_wide)
        l_scr[...] = l_scr[...] + lane_fold(p, jnp.add)
        acc_scr[...] = acc_scr[...] + _dot(p.astype(BF16), key_tile(kt)[:, :KV_LORA])
        return carry

    lax.fori_loop(0, qi + 1, value_body, 0)
    inv_l = 1.0 / jnp.broadcast_to(jnp.sum(l_scr[...], axis=-1, keepdims=True), (m_rows, LANES))
    o = acc_scr[...] * wide(inv_l, KV_LORA // LANES)
    o_ref[...] = o.reshape(MLA_HEADS, tq, KV_LORA).astype(BF16)


def _mla_prompt(qcat, kcat, kmeta, n_seq, tq):
    t = kcat.shape[0]
    seq = t // n_seq
    nq = seq // tq
    m_rows = MLA_HEADS * tq
    return pl.pallas_call(
        functools.partial(_mla_prompt_kernel, tq=tq),
        grid=(n_seq, nq),
        in_specs=[pl.BlockSpec((MLA_HEADS, tq, QK_WIDTH), lambda b, i: (0, b * nq + i, 0)),
                  pl.BlockSpec((seq, QK_WIDTH), lambda b, i: (b, 0)),
                  _full_spec(kmeta)],
        out_specs=pl.BlockSpec((MLA_HEADS, tq, KV_LORA), lambda b, i: (0, b * nq + i, 0)),
        out_shape=jax.ShapeDtypeStruct((MLA_HEADS, t, KV_LORA), BF16),
        scratch_shapes=[pltpu.VMEM((nq, m_rows, tq), F32),
                        pltpu.VMEM((m_rows, LANES), F32),
                        pltpu.VMEM((m_rows, LANES), F32),
                        pltpu.VMEM((m_rows, LANES), F32),
                        pltpu.VMEM((m_rows, KV_LORA), F32)],
        compiler_params=pltpu.CompilerParams(dimension_semantics=("parallel", "arbitrary")),
        name="mla_prompt",
    )(qcat, kcat, kmeta)


Q_ROWS = 8
DEC_GROUP = 8


def _mla_decode_kernel(pt_ref, q_ref, kn_ref, cn_ref, ckv_hbm, kpe_hbm, o_ref,
                       ckv_buf, kpe_buf, sem, *, pg, nc, n_seq):
    s_id = pl.program_id(0)

    def start_chunk(seq_i, ci):
        for i in range(pg):
            page = pt_ref[seq_i, ci * pg + i]
            pltpu.make_async_copy(ckv_hbm.at[page], ckv_buf.at[ci, i], sem.at[0, ci]).start()
            pltpu.make_async_copy(kpe_hbm.at[page], kpe_buf.at[ci, i], sem.at[1, ci]).start()

    def wait_chunk(ci):
        pltpu.make_async_copy(ckv_hbm.at[pl.ds(0, pg)], ckv_buf.at[ci], sem.at[0, ci]).wait()
        pltpu.make_async_copy(kpe_hbm.at[pl.ds(0, pg)], kpe_buf.at[ci], sem.at[1, ci]).wait()

    @pl.when(s_id == 0)
    def _():
        for ci in range(nc):
            start_chunk(0, ci)

    nxt = jnp.minimum(s_id + 1, n_seq - 1)

    q = q_ref[0]
    q_pe = q[:, KV_LORA:KV_LORA + MLA_ROPE]
    q_rows = jnp.concatenate([q[:, :KV_LORA].astype(F32), jnp.zeros((LANES - Q_ROWS, KV_LORA), F32)], axis=0)
    q_t = q_rows.T.astype(BF16)
    group_keys = DEC_GROUP * PAGE_SIZE
    n_groups = pg // DEC_GROUP
    m = jnp.full((Q_ROWS, 1), NEG_BIG, F32)
    l = jnp.zeros((Q_ROWS, 1), F32)
    acc = jnp.zeros((Q_ROWS, KV_LORA), F32)
    for ci in range(nc):
        wait_chunk(ci)
        ckv_groups, parts = [], []
        for g in range(n_groups):
            ckv_g = ckv_buf[ci, g * DEC_GROUP:(g + 1) * DEC_GROUP].reshape(group_keys, KV_LORA).astype(BF16)
            ckv_groups.append(ckv_g)
            s_lat_t = _dot(ckv_g, q_t)
            for i in range(DEC_GROUP):
                s_lat = s_lat_t[i * PAGE_SIZE:(i + 1) * PAGE_SIZE].T[:Q_ROWS]
                kpe_t = kpe_buf[ci, g * DEC_GROUP + i].astype(BF16)
                parts.append(s_lat + _dot(q_pe, kpe_t))
        s = jnp.concatenate(parts, axis=-1)
        m_new = jnp.maximum(m, jnp.max(s, axis=-1, keepdims=True))
        corr = jnp.exp2(m - m_new)
        pb = jnp.exp2(s - m_new)
        l = l * corr + jnp.sum(pb, axis=-1, keepdims=True)
        pb = pb.astype(BF16)
        pv = _dot(pb[:, :group_keys], ckv_groups[0])
        for g in range(1, n_groups):
            pv = pv + _dot(pb[:, g * group_keys:(g + 1) * group_keys], ckv_groups[g])
        acc = acc * corr + pv
        m = m_new
        start_chunk(nxt, ci)

    s_self = jnp.sum(q.astype(F32) * kn_ref[0].astype(F32), axis=-1, keepdims=True)
    m_fin = jnp.maximum(m, s_self)
    corr = jnp.exp2(m - m_fin)
    p_self = jnp.exp2(s_self - m_fin)
    l_fin = l * corr + p_self
    o_ref[0] = (acc * corr + p_self * cn_ref[0]) * (1.0 / l_fin)

    @pl.when(s_id == n_seq - 1)
    def _():
        for ci in range(nc):
            wait_chunk(ci)


def _mla_decode(page_table, qd, kn, cn, cache_ckv, cache_kpe_t, pg):
    n_seq, n_pages = page_table.shape
    nc = n_pages // pg
    grid_spec = pltpu.PrefetchScalarGridSpec(
        num_scalar_prefetch=1,
        grid=(n_seq,),
        in_specs=[pl.BlockSpec((1, Q_ROWS, QK_WIDTH), lambda s, pt: (s, 0, 0)),
                  pl.BlockSpec((1, 1, QK_WIDTH), lambda s, pt: (s, 0, 0)),
                  pl.BlockSpec((1, 1, KV_LORA), lambda s, pt: (s, 0, 0)),
                  pl.BlockSpec(memory_space=pl.ANY),
                  pl.BlockSpec(memory_space=pl.ANY)],
        out_specs=pl.BlockSpec((1, Q_ROWS, KV_LORA), lambda s, pt: (s, 0, 0)),
        scratch_shapes=[pltpu.VMEM((nc, pg, PAGE_SIZE, KV_LORA), F32),
                        pltpu.VMEM((nc, pg, MLA_ROPE, PAGE_SIZE), F32),
                        pltpu.SemaphoreType.DMA((2, nc))])
    return pl.pallas_call(
        functools.partial(_mla_decode_kernel, pg=pg, nc=nc, n_seq=n_seq),
        grid_spec=grid_spec,
        out_shape=jax.ShapeDtypeStruct((n_seq, Q_ROWS, KV_LORA), F32),
        compiler_params=pltpu.CompilerParams(dimension_semantics=("arbitrary",)),
        name="mla_decode",
    )(page_table, qd, kn, cn, cache_ckv, cache_kpe_t)


def _ffn_kernel(x_ref, og_ref, ol_ref, wuv_ref, wout_ref, nffn_ref, wg_ref, wu_ref, wd_ref, nfin_ref, y_ref):
    om = [_dot(ol_ref[h], wuv_ref[h]).astype(BF16) for h in range(MLA_HEADS)]
    cat = jnp.concatenate([og_ref[...]] + om, axis=-1)
    h1 = x_ref[...] + _dot(cat, wout_ref[...])
    n = _rms(h1, nffn_ref[...]).astype(BF16)
    gate = _dot(n, wg_ref[...])
    up = _dot(n, wu_ref[...])
    act = (gate * jax.nn.sigmoid(gate) * up).astype(BF16)
    h2 = h1 + _dot(act, wd_ref[...])
    y_ref[...] = _rms(h2, nfin_ref[...])


def _ffn(x, og, olat, w, tm):
    t = x.shape[0]
    row = lambda width: pl.BlockSpec((tm, width), lambda i: (i, 0))
    weights = (w["w_uv"], w["w_out"], w["norm_ffn"], w["w_gate"], w["w_up"], w["w_down"], w["norm_final"])
    return pl.pallas_call(
        _ffn_kernel,
        grid=(t // tm,),
        in_specs=[row(D_MODEL), row(512), pl.BlockSpec((MLA_HEADS, tm, KV_LORA), lambda i: (0, i, 0))]
                 + [_full_spec(a) for a in weights],
        out_specs=row(D_MODEL),
        out_shape=jax.ShapeDtypeStruct((t, D_MODEL), F32),
        compiler_params=pltpu.CompilerParams(dimension_semantics=("parallel",)),
        name="ffn",
    )(x, og, olat, *weights)


def _prep_weights(norm_mix, w_in, w_gk, b_gk, gla_norm, q_norm, kv_norm, w_uq, w_ukv, w_out,
                  norm_ffn, w_gate, w_up, w_down, norm_final):
    hk = GLA_HEADS * GLA_DK
    hv = GLA_HEADS * GLA_DV
    sizes = (hk, hk, hv, GLA_GATE_RANK, hv, Q_LORA, KV_LORA, MLA_ROPE)
    bounds = [0]
    for sz in sizes:
        bounds.append(bounds[-1] + sz)
    wq, wk, wv, wgr, wg, wcq, wckv, wkpe = (w_in[0][:, bounds[i]:bounds[i + 1]] for i in range(8))
    pad = jnp.zeros((D_MODEL, _IN_PACKED - _OFF_TAIL - MLA_ROPE - GLA_GATE_RANK), w_in.dtype)
    w_in_p = jnp.concatenate([wq, wk, wv, wg, wcq, wckv, wkpe, wgr, pad], axis=1).astype(BF16)

    w_gk_p = jnp.zeros((LANES, hk), F32).at[_TAIL_GR:_TAIL_GR + GLA_GATE_RANK].set(w_gk[0]).astype(BF16)

    wuq = w_uq[0].reshape(Q_LORA, MLA_HEADS, MLA_NOPE + MLA_ROPE)
    wuq_nope = wuq[..., :MLA_NOPE].reshape(Q_LORA, MLA_HEADS * MLA_NOPE)
    wuq_rope = jnp.pad(wuq[..., MLA_NOPE:], ((0, 0), (0, 0), (0, LANES - MLA_ROPE)))
    w_uq_p = jnp.concatenate([wuq_nope, wuq_rope.reshape(Q_LORA, MLA_HEADS * LANES)], axis=1).astype(BF16)

    wukv = w_ukv[0].reshape(KV_LORA, MLA_HEADS, MLA_NOPE + MLA_V)
    w_ukt = jnp.transpose(wukv[..., :MLA_NOPE], (1, 2, 0)).astype(BF16)
    w_uv = jnp.transpose(wukv[..., MLA_NOPE:], (1, 0, 2)).astype(BF16)

    r = lambda a: a.reshape(1, -1).astype(F32)
    return dict(norm_mix=r(norm_mix[0]), w_in=w_in_p, w_gk=w_gk_p, b_gk=r(b_gk[0]), q_norm=r(q_norm[0]),
                kv_norm=r(kv_norm[0]), w_uq=w_uq_p, w_ukt=w_ukt, w_uv=w_uv, gla_norm=r(gla_norm[0]),
                w_out=w_out[0].astype(BF16), norm_ffn=r(norm_ffn[0]), w_gate=w_gate[0].astype(BF16),
                w_up=w_up[0].astype(BF16), w_down=w_down[0].astype(BF16), norm_final=r(norm_final))


def _rope_tables(pos):
    half = MLA_ROPE // 2
    inv = ROPE_THETA ** (-jnp.arange(half, dtype=F32) / half)
    ang = pos.astype(F32)[:, None] * inv[None, :]
    cos, sin = jnp.cos(ang), jnp.sin(ang)
    z = jnp.zeros((pos.shape[0], LANES - MLA_ROPE), F32)
    return jnp.concatenate([cos, cos, z], axis=1), jnp.concatenate([-sin, sin, z], axis=1)


def kernel(x_prompt, x_sample, cache_ckv, cache_kpe, state_gla, page_table, meta_tokens, norm_mix, w_in, w_gk, b_gk, gla_norm, q_norm, kv_norm, w_uq, w_ukv, w_out, norm_ffn, w_gate, w_up, w_down, norm_final):
    n_b, seq, d = x_prompt.shape
    n_dec, t_dec, _ = x_sample.shape
    assert w_in.shape[0] == 1 and t_dec == 1 and d == D_MODEL
    n_pages = page_table.shape[1]
    past = n_pages * PAGE_SIZE
    hk = GLA_HEADS * GLA_DK
    w = _prep_weights(norm_mix, w_in, w_gk, b_gk, gla_norm, q_norm, kv_norm, w_uq, w_ukv, w_out,
                      norm_ffn, w_gate, w_up, w_down, norm_final)

    small_rows = 2 * LANES
    assert N_META <= LANES and n_dec == LANES
    xs = x_sample[:, 0]
    x_small = jnp.concatenate([meta_tokens.astype(F32), jnp.zeros((LANES - N_META, d), F32), xs], axis=0)
    pos_small = jnp.concatenate([jnp.arange(N_META), jnp.zeros((LANES - N_META,), jnp.int32),
                                 jnp.full((n_dec,), past, jnp.int32)])
    sm = _project(x_small, *_rope_tables(pos_small), w, small_rows)
    xp = x_prompt.reshape(n_b * seq, d)
    pr = _project(xp, *_rope_tables(N_META + jnp.arange(seq)), w, 512)

    gla_in = ("gq", "gk", "lg", "gv", "g")
    _, s_meta = _gla_scan(*(sm[n][:N_META] for n in gla_in), w["gla_norm"],
                          jnp.zeros((1, hk, GLA_DV), F32), 1, GLA_STEP)
    og_p, s_prompt = _gla_scan(*(pr[n] for n in gla_in), w["gla_norm"], s_meta, n_b, 256)
    og_s, s_sample = _gla_token(*(sm[n][LANES:] for n in gla_in), w["gla_norm"],
                                state_gla[0].reshape(n_dec, hk, GLA_DV))

    kmeta = jnp.pad(sm["kcat"][:N_META], ((0, LANES - N_META), (0, 0)))
    olat_p = _mla_prompt(pr["qcat"], pr["kcat"], kmeta, n_b, 512)
    qd = jnp.pad(jnp.transpose(sm["qcat"][:, LANES:], (1, 0, 2)), ((0, 0), (0, Q_ROWS - MLA_HEADS), (0, 0)))
    cache_kpe_t = jnp.swapaxes(cache_kpe, 2, 3).reshape(-1, MLA_ROPE, PAGE_SIZE)
    o_dec = _mla_decode(page_table, qd, sm["kcat"][LANES:, None, :], sm["ckv"][LANES:, None, :],
                        cache_ckv.reshape(-1, PAGE_SIZE, KV_LORA), cache_kpe_t, 32)
    olat_s = jnp.transpose(o_dec[:, :MLA_HEADS], (1, 0, 2)).astype(BF16)

    y_prompt = _ffn(xp, og_p, olat_p, w, 512).reshape(n_b, seq, d)
    y_sample = _ffn(xs, og_s, olat_s, w, LANES).reshape(n_dec, 1, d)

    bcast = lambda a: jnp.broadcast_to(a[None, :N_META], (n_b, N_META, a.shape[-1]))
    ckv_prompt = jnp.concatenate([bcast(sm["ckv"]), pr["ckv"].reshape(n_b, seq, KV_LORA)], axis=1)[None]
    kpe_prompt = jnp.concatenate([bcast(sm["kpe"]), pr["kpe"].reshape(n_b, seq, MLA_ROPE)], axis=1)[None]
    gla_prompt = s_prompt.reshape(1, n_b, GLA_HEADS, GLA_DK, GLA_DV)
    ckv_sample = sm["ckv"][LANES:].reshape(1, n_dec, 1, KV_LORA)
    kpe_sample = sm["kpe"][LANES:].reshape(1, n_dec, 1, MLA_ROPE)
    gla_sample = s_sample.reshape(1, n_dec, GLA_HEADS, GLA_DK, GLA_DV)
    return (y_prompt, y_sample, ckv_prompt, kpe_prompt, gla_prompt, ckv_sample, kpe_sample, gla_sample)
```

```python
import functools

import jax
import jax.numpy as jnp
from jax import lax
from jax.experimental import pallas as pl
from jax.experimental.pallas import tpu as pltpu

F32 = jnp.float32
BF16 = jnp.bfloat16

D_MODEL = 1024
N_META = 16
EPS = 1e-6
GLA_HEADS = 4
GLA_DV = 128
GLA_DK = 64
GLA_GATE_RANK = 16
GLA_GATE_NORM = 16.0
MLA_HEADS = 4
MLA_NOPE = 128
MLA_ROPE = 64
MLA_V = 128
Q_LORA = 384
KV_LORA = 256
ROPE_THETA = 10000.0
MLA_SCALE = (MLA_NOPE + MLA_ROPE) ** -0.5
PAGE_SIZE = 128
D_FF = 2816

LANES = 128
GLA_STEP = 16
GLA_CHUNK = 64
GLA_SAFE_LOG_DECAY = 40.0
QK_WIDTH = KV_LORA + LANES
NEG_BIG = -1e30
Q_PRESCALE = MLA_SCALE * 1.4426950408889634

_OFF_Q, _OFF_K, _OFF_V, _OFF_G, _OFF_CQ, _OFF_CKV, _OFF_TAIL, _IN_PACKED = (
    0, 256, 512, 1024, 1536, 1920, 2176, 2304)
_TAIL_GR = MLA_ROPE


def _rms(x, w):
    return x * lax.rsqrt(jnp.mean(x * x, axis=-1, keepdims=True) + EPS) * w


def _rope_tile(x, c, s):
    lane = lax.broadcasted_iota(jnp.int32, x.shape, 1)
    from_right = pltpu.roll(x, LANES - MLA_ROPE // 2, 1)
    from_left = pltpu.roll(x, MLA_ROPE // 2, 1)
    swapped = jnp.where(lane < MLA_ROPE // 2, from_right, from_left)
    return x * c + swapped * s


def _dot(a, b):
    return jnp.dot(a, b, preferred_element_type=F32)


def _dot_nt(a, b):
    return lax.dot_general(a, b, (((1,), (1,)), ((), ())), preferred_element_type=F32)


PROJ_SUB_ROWS = 256


def _proj_kernel(x_ref, cos_ref, sin_ref, nmix_ref, win_ref, wgk_ref, bgk_ref, qn_ref, kvn_ref,
                 wuq_ref, wukt_ref,
                 gq_ref, gk_ref, gv_ref, lg_ref, g_ref, qcat_ref, kcat_ref, ckv_ref, kpe_ref):
    tm = x_ref.shape[0]
    sub = min(tm, PROJ_SUB_ROWS)
    for u in range(tm // sub):
        rs = slice(u * sub, (u + 1) * sub)
        n = _rms(x_ref[rs, :], nmix_ref[...]).astype(BF16)
        proj = _dot(n, win_ref[...])
        gq_ref[rs, :] = proj[:, _OFF_Q:_OFF_K] * (GLA_DK ** -0.5)
        gk_ref[rs, :] = proj[:, _OFF_K:_OFF_V]
        gv_ref[rs, :] = proj[:, _OFF_V:_OFF_G]
        g_ref[rs, :] = proj[:, _OFF_G:_OFF_CQ]
        tail = proj[:, _OFF_TAIL:_IN_PACKED]
        z = _dot(tail.astype(BF16), wgk_ref[...]) + bgk_ref[...]
        lg_ref[rs, :] = jax.nn.log_sigmoid(z) * (1.0 / GLA_GATE_NORM)

        cos = cos_ref[rs, :]
        sin = sin_ref[rs, :]
        cq = _rms(proj[:, _OFF_CQ:_OFF_CKV], qn_ref[...]).astype(BF16)
        qh = _dot(cq, wuq_ref[...])
        for h in range(MLA_HEADS):
            q_nope = qh[:, h * MLA_NOPE:(h + 1) * MLA_NOPE].astype(BF16)
            q_lat = _dot(q_nope, wukt_ref[h])
            base = MLA_HEADS * MLA_NOPE + h * LANES
            q_pe = _rope_tile(qh[:, base:base + LANES], cos, sin)
            qcat_ref[h, rs, :] = (jnp.concatenate([q_lat, q_pe], axis=-1) * Q_PRESCALE).astype(BF16)

        ckv = _rms(proj[:, _OFF_CKV:_OFF_TAIL], kvn_ref[...])
        ckv_ref[rs, :] = ckv
        k_pe = _rope_tile(tail, cos, sin)
        kpe_ref[rs, :] = k_pe[:, :MLA_ROPE]
        kcat_ref[rs, :] = jnp.concatenate([ckv, k_pe], axis=-1).astype(BF16)


def _full_spec(a):
    nd = a.ndim
    return pl.BlockSpec(a.shape, lambda *_: (0,) * nd)


def _project(x, cos, sin, w, tm):
    t = x.shape[0]
    n_pos_blocks = cos.shape[0] // tm
    row = lambda width: pl.BlockSpec((tm, width), lambda i: (i, 0))
    pos = pl.BlockSpec((tm, LANES), lambda i: (i % n_pos_blocks, 0))
    weights = (w["norm_mix"], w["w_in"], w["w_gk"], w["b_gk"], w["q_norm"], w["kv_norm"],
               w["w_uq"], w["w_ukt"])
    out_shape = (
        jax.ShapeDtypeStruct((t, 256), F32), jax.ShapeDtypeStruct((t, 256), F32),
        jax.ShapeDtypeStruct((t, 512), F32), jax.ShapeDtypeStruct((t, 256), F32),
        jax.ShapeDtypeStruct((t, 512), F32),
        jax.ShapeDtypeStruct((MLA_HEADS, t, QK_WIDTH), BF16),
        jax.ShapeDtypeStruct((t, QK_WIDTH), BF16),
        jax.ShapeDtypeStruct((t, KV_LORA), F32), jax.ShapeDtypeStruct((t, MLA_ROPE), F32))
    out_specs = (row(256), row(256), row(512), row(256), row(512),
                 pl.BlockSpec((MLA_HEADS, tm, QK_WIDTH), lambda i: (0, i, 0)),
                 row(QK_WIDTH), row(KV_LORA), row(MLA_ROPE))
    names = ("gq", "gk", "gv", "lg", "g", "qcat", "kcat", "ckv", "kpe")
    outs = pl.pallas_call(
        _proj_kernel,
        grid=(t // tm,),
        in_specs=[row(D_MODEL), pos, pos] + [_full_spec(a) for a in weights],
        out_specs=out_specs,
        out_shape=out_shape,
        compiler_params=pltpu.CompilerParams(dimension_semantics=("parallel",)),
        name="project",
    )(x, cos, sin, *weights)
    return dict(zip(names, outs))


def _cum_log_decay(lg):
    n = lg.shape[0]
    tri = (lax.broadcasted_iota(jnp.int32, (n, n), 0) >= lax.broadcasted_iota(jnp.int32, (n, n), 1)).astype(F32)
    return jnp.dot(tri, lg, precision=lax.Precision.HIGHEST, preferred_element_type=F32)


def _head_masked_rows(x, pad_rows=0):
    lane_head = lax.broadcasted_iota(jnp.int32, x.shape, 1) // GLA_DK
    blocks = [jnp.where(lane_head == h, x, 0.0) for h in range(GLA_HEADS)]
    if pad_rows:
        blocks.append(jnp.zeros((pad_rows, x.shape[1]), x.dtype))
    return jnp.concatenate(blocks, axis=0)


def _head_stacked_values(v, pad_rows=0):
    blocks = [v[:, h * GLA_DV:(h + 1) * GLA_DV] for h in range(GLA_HEADS)]
    if pad_rows:
        blocks.append(jnp.zeros((pad_rows, GLA_DV), v.dtype))
    return jnp.concatenate(blocks, axis=0)


def _gla_state_update(state, kt, v, b_last, pad_rows=0):
    kbd = _head_masked_rows(kt, pad_rows)
    upd = _dot(kbd.T.astype(BF16), _head_stacked_values(v, pad_rows).astype(BF16))
    dec = jnp.broadcast_to(jnp.exp(b_last), (LANES, kt.shape[1])).T
    return dec * state + upd


def _gla_gate_out(o_heads, g, gn, dtype=BF16):
    normed = jnp.concatenate([_rms(o, gn) for o in o_heads], axis=-1)
    return (normed * (g * jax.nn.sigmoid(g))).astype(dtype)


def _gla_exact_step(state, q, k, lg, v, n_src=None):
    c = q.shape[0]
    n_src = c if n_src is None else n_src
    half = lax.broadcasted_iota(jnp.int32, (c, LANES), 1) // GLA_DK
    row_t = lax.broadcasted_iota(jnp.int32, q.shape, 0)
    b = _cum_log_decay(lg)
    b_last = b[c - 1:c, :]

    o_st = _dot(_head_masked_rows(q * jnp.exp(b)).astype(BF16), state.astype(BF16))
    o_heads = [o_st[h * c:(h + 1) * c] for h in range(GLA_HEADS)]

    for s in range(n_src):
        decay = jnp.exp(jnp.minimum(b - b[s:s + 1, :], 0.0))
        w = jnp.where(row_t >= s, q * k[s:s + 1, :] * decay, 0.0)
        for h in range(GLA_HEADS):
            tile = w[:, (h // 2) * LANES:(h // 2 + 1) * LANES]
            a = jnp.sum(jnp.where(half == h % 2, tile, 0.0), axis=1, keepdims=True)
            o_heads[h] = o_heads[h] + a * v[s:s + 1, h * GLA_DV:(h + 1) * GLA_DV]

    state = _gla_state_update(state, k * jnp.exp(b_last - b), v, b_last, LANES - GLA_HEADS * c)
    return state, o_heads


def _gla_matmul_chunk(state, q, k, b, v):
    c = GLA_CHUNK
    b_last = b[c - 1:c, :]
    qm = _head_masked_rows(q * jnp.exp(b)).astype(BF16)
    k_inv = (k * jnp.exp(-b)).astype(BF16)
    a = _dot_nt(qm, k_inv)
    row_t = lax.broadcasted_iota(jnp.int32, a.shape, 0) % c
    col_s = lax.broadcasted_iota(jnp.int32, a.shape, 1)
    a = jnp.where(col_s <= row_t, a, 0.0).astype(BF16)
    o_st = _dot(qm, state.astype(BF16))
    vb = v.astype(BF16)
    o_heads = [o_st[h * c:(h + 1) * c] + _dot(a[h * c:(h + 1) * c], vb[:, h * GLA_DV:(h + 1) * GLA_DV])
               for h in range(GLA_HEADS)]
    state = _gla_state_update(state, k * jnp.exp(b_last - b), v, b_last)
    return state, o_heads


def _gla_kernel(q_ref, k_ref, lg_ref, v_ref, g_ref, gn_ref, s0_ref, og_ref, sout_ref, s_scr, *, tt):
    j = pl.program_id(1)

    @pl.when(j == 0)
    def _():
        s_scr[...] = s0_ref[0]

    gn = gn_ref[...]

    def exact_steps(start, n_steps):
        state = s_scr[...]
        for u in range(n_steps):
            sl = slice(start + u * GLA_STEP, start + (u + 1) * GLA_STEP)
            state, o_heads = _gla_exact_step(state, q_ref[sl, :], k_ref[sl, :], lg_ref[sl, :], v_ref[sl, :])
            og_ref[sl, :] = _gla_gate_out(o_heads, g_ref[sl, :], gn)
        s_scr[...] = state

    if tt % GLA_CHUNK == 0:
        chunks = [slice(ci * GLA_CHUNK, (ci + 1) * GLA_CHUNK) for ci in range(tt // GLA_CHUNK)]
        bs = [_cum_log_decay(lg_ref[sl, :]) for sl in chunks]
        worst = bs[0][GLA_CHUNK - 1:GLA_CHUNK, :]
        for b in bs[1:]:
            worst = jnp.minimum(worst, b[GLA_CHUNK - 1:GLA_CHUNK, :])
        mild = jnp.max(-worst) <= GLA_SAFE_LOG_DECAY

        @pl.when(mild)
        def _():
            state = s_scr[...]
            for sl, b in zip(chunks, bs):
                state, o_heads = _gla_matmul_chunk(state, q_ref[sl, :], k_ref[sl, :], b, v_ref[sl, :])
                og_ref[sl, :] = _gla_gate_out(o_heads, g_ref[sl, :], gn)
            s_scr[...] = state

        @pl.when(jnp.logical_not(mild))
        def _():
            exact_steps(0, tt // GLA_STEP)
    else:
        exact_steps(0, tt // GLA_STEP)

    @pl.when(j == pl.num_programs(1) - 1)
    def _():
        sout_ref[0] = s_scr[...]


def _gla_scan(q, k, lg, v, g, gla_norm, s0, n_seq, tt):
    t = q.shape[0]
    n_tiles = t // n_seq // tt
    row = lambda width: pl.BlockSpec((tt, width), lambda b, j: (b * n_tiles + j, 0))
    s0_map = (lambda b, j: (b, 0, 0)) if s0.shape[0] == n_seq else (lambda b, j: (0, 0, 0))
    hk = GLA_HEADS * GLA_DK
    og, s_out = pl.pallas_call(
        functools.partial(_gla_kernel, tt=tt),
        grid=(n_seq, n_tiles),
        in_specs=[row(256), row(256), row(256), row(512), row(512), _full_spec(gla_norm),
                  pl.BlockSpec((1, hk, GLA_DV), s0_map)],
        out_specs=(row(512), pl.BlockSpec((1, hk, GLA_DV), lambda b, j: (b, 0, 0))),
        out_shape=(jax.ShapeDtypeStruct((t, 512), BF16),
                   jax.ShapeDtypeStruct((n_seq, hk, GLA_DV), F32)),
        scratch_shapes=[pltpu.VMEM((hk, GLA_DV), F32)],
        compiler_params=pltpu.CompilerParams(dimension_semantics=("parallel", "arbitrary")),
        name="gla_scan",
    )(q, k, lg, v, g, gla_norm, s0)
    return og, s_out


TOKEN_ROWS = 8
TOKEN_SEQS = 8


def _gla_token_kernel(q_ref, k_ref, lg_ref, v_ref, g_ref, gn_ref, s0_ref, og_ref, sout_ref):
    gn = gn_ref[...]
    first = lax.broadcasted_iota(jnp.int32, (TOKEN_ROWS, 1), 0) == 0
    tile = lambda ref, i: jnp.where(first, ref[i:i + 1, :], 0.0)
    rows = []
    for i in range(TOKEN_SEQS):
        state, o_heads = _gla_exact_step(s0_ref[i], tile(q_ref, i), tile(k_ref, i), tile(lg_ref, i),
                                         tile(v_ref, i), n_src=1)
        rows.append(_gla_gate_out(o_heads, tile(g_ref, i), gn, F32)[0:1])
        sout_ref[i] = state
    og_ref[...] = jnp.concatenate(rows, axis=0)


def _gla_token(q, k, lg, v, g, gla_norm, s0):
    n = q.shape[0]
    hk = GLA_HEADS * GLA_DK
    row = lambda width: pl.BlockSpec((TOKEN_SEQS, width), lambda i: (i, 0))
    st = pl.BlockSpec((TOKEN_SEQS, hk, GLA_DV), lambda i: (i, 0, 0))
    return pl.pallas_call(
        _gla_token_kernel,
        grid=(n // TOKEN_SEQS,),
        in_specs=[row(256), row(256), row(256), row(512), row(512), _full_spec(gla_norm), st],
        out_specs=(row(512), st),
        out_shape=(jax.ShapeDtypeStruct((n, 512), F32),
                   jax.ShapeDtypeStruct((n, hk, GLA_DV), F32)),
        compiler_params=pltpu.CompilerParams(dimension_semantics=("parallel",)),
        name="gla_token",
    )(q, k, lg, v, g, gla_norm, s0)


def _mla_prompt_kernel(q_ref, k_ref, km_ref, o_ref, s_scr, sm_scr, mx_scr, l_scr, acc_scr, *, tq):
    qi = pl.program_id(1)
    m_rows = MLA_HEADS * tq
    n_lane_tiles = tq // LANES
    q = q_ref[...].reshape(m_rows, QK_WIDTH)
    lane_fold = lambda a, op: functools.reduce(
        op, [a[:, i * LANES:(i + 1) * LANES] for i in range(a.shape[1] // LANES)])
    wide = lambda a, n: jnp.concatenate([a] * n, axis=-1)
    key_tile = lambda kt: k_ref[pl.ds(pl.multiple_of(kt * tq, tq), tq), :]

    km = km_ref[...]
    s = _dot_nt(q, km)
    col = lax.broadcasted_iota(jnp.int32, s.shape, 1)
    s = jnp.where(col < N_META, s, NEG_BIG)
    sm_scr[...] = s
    mx_scr[...] = s

    def score_tile(kt, masked):
        s = _dot_nt(q, key_tile(kt))
        if masked:
            row = lax.broadcasted_iota(jnp.int32, s.shape, 0) % tq
            col = lax.broadcasted_iota(jnp.int32, s.shape, 1)
            s = jnp.where(col <= row, s, NEG_BIG)
        s_scr[kt] = s
        mx_scr[...] = jnp.maximum(mx_scr[...], lane_fold(s, jnp.maximum))

    def score_body(kt, carry):
        score_tile(kt, False)
        return carry

    lax.fori_loop(0, qi, score_body, 0)
    score_tile(qi, True)
    m = jnp.broadcast_to(jnp.max(mx_scr[...], axis=-1, keepdims=True), (m_rows, LANES))

    p = jnp.exp2(sm_scr[...] - m)
    l_scr[...] = p
    acc_scr[...] = _dot(p.astype(BF16), km[:, :KV_LORA])
    m_wide = wide(m, n_lane_tiles)

    def value_body(kt, carry):
        p = jnp.exp2(s_scr[kt] - m_wide)
        l_scr[...] = l_scr[...] + lane_fold(p, jnp.add)
        acc_scr[...] = acc_scr[...] + _dot(p.astype(BF16), key_tile(kt)[:, :KV_LORA])
        return carry

    lax.fori_loop(0, qi + 1, value_body, 0)
    inv_l = 1.0 / jnp.broadcast_to(jnp.sum(l_scr[...], axis=-1, keepdims=True), (m_rows, LANES))
    o = acc_scr[...] * wide(inv_l, KV_LORA // LANES)
    o_ref[...] = o.reshape(MLA_HEADS, tq, KV_LORA).astype(BF16)


def _mla_prompt(qcat, kcat, kmeta, n_seq, tq):
    t = kcat.shape[0]
    seq = t // n_seq
    nq = seq // tq
    m_rows = MLA_HEADS * tq
    return pl.pallas_call(
        functools.partial(_mla_prompt_kernel, tq=tq),
        grid=(n_seq, nq),
        in_specs=[pl.BlockSpec((MLA_HEADS, tq, QK_WIDTH), lambda b, i: (0, b * nq + i, 0)),
                  pl.BlockSpec((seq, QK_WIDTH), lambda b, i: (b, 0)),
                  _full_spec(kmeta)],
        out_specs=pl.BlockSpec((MLA_HEADS, tq, KV_LORA), lambda b, i: (0, b * nq + i, 0)),
        out_shape=jax.ShapeDtypeStruct((MLA_HEADS, t, KV_LORA), BF16),
        scratch_shapes=[pltpu.VMEM((nq, m_rows, tq), F32),
                        pltpu.VMEM((m_rows, LANES), F32),
                        pltpu.VMEM((m_rows, LANES), F32),
                        pltpu.VMEM((m_rows, LANES), F32),
                        pltpu.VMEM((m_rows, KV_LORA), F32)],
        compiler_params=pltpu.CompilerParams(dimension_semantics=("parallel", "arbitrary")),
        name="mla_prompt",
    )(qcat, kcat, kmeta)


Q_ROWS = 8
DEC_GROUP = 8


def _mla_decode_kernel(pt_ref, q_ref, kn_ref, cn_ref, ckv_hbm, kpe_hbm, o_ref,
                       ckv_buf, kpe_buf, sem, *, pg, nc, n_seq):
    s_id = pl.program_id(0)

    def start_chunk(seq_i, ci):
        for i in range(pg):
            page = pt_ref[seq_i, ci * pg + i]
            pltpu.make_async_copy(ckv_hbm.at[page], ckv_buf.at[ci, i], sem.at[0, ci]).start(priority=i % 2)
            pltpu.make_async_copy(kpe_hbm.at[page], kpe_buf.at[ci, i], sem.at[1, ci]).start(priority=i % 2)

    def wait_chunk(ci):
        pltpu.make_async_copy(ckv_hbm.at[pl.ds(0, pg)], ckv_buf.at[ci], sem.at[0, ci]).wait()
        pltpu.make_async_copy(kpe_hbm.at[pl.ds(0, pg)], kpe_buf.at[ci], sem.at[1, ci]).wait()

    @pl.when(s_id == 0)
    def _():
        for ci in range(nc):
            start_chunk(0, ci)

    nxt = jnp.minimum(s_id + 1, n_seq - 1)

    q = q_ref[0]
    q_pe = q[:, KV_LORA:KV_LORA + MLA_ROPE]
    q_rows = jnp.concatenate([q[:, :KV_LORA].astype(F32), jnp.zeros((LANES - Q_ROWS, KV_LORA), F32)], axis=0)
    q_t = q_rows.T.astype(BF16)
    group_keys = DEC_GROUP * PAGE_SIZE
    n_groups = pg // DEC_GROUP
    m = jnp.full((Q_ROWS, 1), NEG_BIG, F32)
    l = jnp.zeros((Q_ROWS, 1), F32)
    acc = jnp.zeros((Q_ROWS, KV_LORA), F32)
    for ci in range(nc):
        wait_chunk(ci)
        ckv_groups, parts = [], []
        for g in range(n_groups):
            ckv_g = ckv_buf[ci, g * DEC_GROUP:(g + 1) * DEC_GROUP].reshape(group_keys, KV_LORA).astype(BF16)
            ckv_groups.append(ckv_g)
            s_lat_t = _dot(ckv_g, q_t)
            for i in range(DEC_GROUP):
                s_lat = s_lat_t[i * PAGE_SIZE:(i + 1) * PAGE_SIZE].T[:Q_ROWS]
                kpe_t = kpe_buf[ci, g * DEC_GROUP + i].astype(BF16)
                parts.append(s_lat + _dot(q_pe, kpe_t))
        s = jnp.concatenate(parts, axis=-1)
        m_new = jnp.maximum(m, jnp.max(s, axis=-1, keepdims=True))
        corr = jnp.exp2(m - m_new)
        pb = jnp.exp2(s - m_new)
        l = l * corr + jnp.sum(pb, axis=-1, keepdims=True)
        pb = pb.astype(BF16)
        pv = _dot(pb[:, :group_keys], ckv_groups[0])
        for g in range(1, n_groups):
            pv = pv + _dot(pb[:, g * group_keys:(g + 1) * group_keys], ckv_groups[g])
        acc = acc * corr + pv
        m = m_new
        start_chunk(nxt, ci)

    s_self = jnp.sum(q.astype(F32) * kn_ref[0].astype(F32), axis=-1, keepdims=True)
    m_fin = jnp.maximum(m, s_self)
    corr = jnp.exp2(m - m_fin)
    p_self = jnp.exp2(s_self - m_fin)
    l_fin = l * corr + p_self
    o_ref[0] = (acc * corr + p_self * cn_ref[0]) * (1.0 / l_fin)

    @pl.when(s_id == n_seq - 1)
    def _():
        for ci in range(nc):
            wait_chunk(ci)


def _mla_decode(page_table, qd, kn, cn, cache_ckv, cache_kpe_t, pg):
    n_seq, n_pages = page_table.shape
    nc = n_pages // pg
    grid_spec = pltpu.PrefetchScalarGridSpec(
        num_scalar_prefetch=1,
        grid=(n_seq,),
        in_specs=[pl.BlockSpec((1, Q_ROWS, QK_WIDTH), lambda s, pt: (s, 0, 0)),
                  pl.BlockSpec((1, 1, QK_WIDTH), lambda s, pt: (s, 0, 0)),
                  pl.BlockSpec((1, 1, KV_LORA), lambda s, pt: (s, 0, 0)),
                  pl.BlockSpec(memory_space=pl.ANY),
                  pl.BlockSpec(memory_space=pl.ANY)],
        out_specs=pl.BlockSpec((1, Q_ROWS, KV_LORA), lambda s, pt: (s, 0, 0)),
        scratch_shapes=[pltpu.VMEM((nc, pg, PAGE_SIZE, KV_LORA), F32),
                        pltpu.VMEM((nc, pg, MLA_ROPE, PAGE_SIZE), F32),
                        pltpu.SemaphoreType.DMA((2, nc))])
    return pl.pallas_call(
        functools.partial(_mla_decode_kernel, pg=pg, nc=nc, n_seq=n_seq),
        grid_spec=grid_spec,
        out_shape=jax.ShapeDtypeStruct((n_seq, Q_ROWS, KV_LORA), F32),
        compiler_params=pltpu.CompilerParams(dimension_semantics=("arbitrary",)),
        name="mla_decode",
    )(page_table, qd, kn, cn, cache_ckv, cache_kpe_t)


def _ffn_kernel(x_ref, og_ref, ol_ref, wuv_ref, wout_ref, nffn_ref, wg_ref, wu_ref, wd_ref, nfin_ref, y_ref):
    om = [_dot(ol_ref[h], wuv_ref[h]).astype(BF16) for h in range(MLA_HEADS)]
    cat = jnp.concatenate([og_ref[...].astype(BF16)] + om, axis=-1)
    h1 = x_ref[...] + _dot(cat, wout_ref[...])
    n = _rms(h1, nffn_ref[...]).astype(BF16)
    gate = _dot(n, wg_ref[...])
    up = _dot(n, wu_ref[...])
    act = (gate * jax.nn.sigmoid(gate) * up).astype(BF16)
    h2 = h1 + _dot(act, wd_ref[...])
    y_ref[...] = _rms(h2, nfin_ref[...])


def _ffn(x, og, olat, w, tm):
    t = x.shape[0]
    row = lambda width: pl.BlockSpec((tm, width), lambda i: (i, 0))
    weights = (w["w_uv"], w["w_out"], w["norm_ffn"], w["w_gate"], w["w_up"], w["w_down"], w["norm_final"])
    return pl.pallas_call(
        _ffn_kernel,
        grid=(t // tm,),
        in_specs=[row(D_MODEL), row(512), pl.BlockSpec((MLA_HEADS, tm, KV_LORA), lambda i: (0, i, 0))]
                 + [_full_spec(a) for a in weights],
        out_specs=row(D_MODEL),
        out_shape=jax.ShapeDtypeStruct((t, D_MODEL), F32),
        compiler_params=pltpu.CompilerParams(dimension_semantics=("parallel",)),
        name="ffn",
    )(x, og, olat, *weights)


def _prep_weights(norm_mix, w_in, w_gk, b_gk, gla_norm, q_norm, kv_norm, w_uq, w_ukv, w_out,
                  norm_ffn, w_gate, w_up, w_down, norm_final):
    hk = GLA_HEADS * GLA_DK
    hv = GLA_HEADS * GLA_DV
    sizes = (hk, hk, hv, GLA_GATE_RANK, hv, Q_LORA, KV_LORA, MLA_ROPE)
    bounds = [0]
    for sz in sizes:
        bounds.append(bounds[-1] + sz)
    wq, wk, wv, wgr, wg, wcq, wckv, wkpe = (w_in[0][:, bounds[i]:bounds[i + 1]] for i in range(8))
    pad = jnp.zeros((D_MODEL, _IN_PACKED - _OFF_TAIL - MLA_ROPE - GLA_GATE_RANK), w_in.dtype)
    w_in_p = jnp.concatenate([wq, wk, wv, wg, wcq, wckv, wkpe, wgr, pad], axis=1).astype(BF16)

    w_gk_p = jnp.zeros((LANES, hk), F32).at[_TAIL_GR:_TAIL_GR + GLA_GATE_RANK].set(w_gk[0]).astype(BF16)

    wuq = w_uq[0].reshape(Q_LORA, MLA_HEADS, MLA_NOPE + MLA_ROPE)
    wuq_nope = wuq[..., :MLA_NOPE].reshape(Q_LORA, MLA_HEADS * MLA_NOPE)
    wuq_rope = jnp.pad(wuq[..., MLA_NOPE:], ((0, 0), (0, 0), (0, LANES - MLA_ROPE)))
    w_uq_p = jnp.concatenate([wuq_nope, wuq_rope.reshape(Q_LORA, MLA_HEADS * LANES)], axis=1).astype(BF16)

    wukv = w_ukv[0].reshape(KV_LORA, MLA_HEADS, MLA_NOPE + MLA_V)
    w_ukt = jnp.transpose(wukv[..., :MLA_NOPE], (1, 2, 0)).astype(BF16)
    w_uv = jnp.transpose(wukv[..., MLA_NOPE:], (1, 0, 2)).astype(BF16)

    r = lambda a: a.reshape(1, -1).astype(F32)
    return dict(norm_mix=r(norm_mix[0]), w_in=w_in_p, w_gk=w_gk_p, b_gk=r(b_gk[0]), q_norm=r(q_norm[0]),
                kv_norm=r(kv_norm[0]), w_uq=w_uq_p, w_ukt=w_ukt, w_uv=w_uv, gla_norm=r(gla_norm[0]),
                w_out=w_out[0].astype(BF16), norm_ffn=r(norm_ffn[0]), w_gate=w_gate[0].astype(BF16),
                w_up=w_up[0].astype(BF16), w_down=w_down[0].astype(BF16), norm_final=r(norm_final))


def _rope_tables(pos):
    half = MLA_ROPE // 2
    inv = ROPE_THETA ** (-jnp.arange(half, dtype=F32) / half)
    ang = pos.astype(F32)[:, None] * inv[None, :]
    cos, sin = jnp.cos(ang), jnp.sin(ang)
    z = jnp.zeros((pos.shape[0], LANES - MLA_ROPE), F32)
    return jnp.concatenate([cos, cos, z], axis=1), jnp.concatenate([-sin, sin, z], axis=1)


def kernel(x_prompt, x_sample, cache_ckv, cache_kpe, state_gla, page_table, meta_tokens, norm_mix, w_in, w_gk, b_gk, gla_norm, q_norm, kv_norm, w_uq, w_ukv, w_out, norm_ffn, w_gate, w_up, w_down, norm_final):
    n_b, seq, d = x_prompt.shape
    n_dec, t_dec, _ = x_sample.shape
    assert w_in.shape[0] == 1 and t_dec == 1 and d == D_MODEL
    n_pages = page_table.shape[1]
    past = n_pages * PAGE_SIZE
    hk = GLA_HEADS * GLA_DK
    w = _prep_weights(norm_mix, w_in, w_gk, b_gk, gla_norm, q_norm, kv_norm, w_uq, w_ukv, w_out,
                      norm_ffn, w_gate, w_up, w_down, norm_final)

    small_rows = 2 * LANES
    assert N_META <= LANES and n_dec == LANES
    xs = x_sample[:, 0]
    x_small = jnp.concatenate([meta_tokens.astype(F32), jnp.zeros((LANES - N_META, d), F32), xs], axis=0)
    pos_small = jnp.concatenate([jnp.arange(N_META), jnp.zeros((LANES - N_META,), jnp.int32),
                                 jnp.full((n_dec,), past, jnp.int32)])
    sm = _project(x_small, *_rope_tables(pos_small), w, small_rows)
    xp = x_prompt.reshape(n_b * seq, d)
    pr = _project(xp, *_rope_tables(N_META + jnp.arange(seq)), w, 512)

    gla_in = ("gq", "gk", "lg", "gv", "g")
    _, s_meta = _gla_scan(*(sm[n][:N_META] for n in gla_in), w["gla_norm"],
                          jnp.zeros((1, hk, GLA_DV), F32), 1, GLA_STEP)
    og_p, s_prompt = _gla_scan(*(pr[n] for n in gla_in), w["gla_norm"], s_meta, n_b, 256)
    og_s, s_sample = _gla_token(*(sm[n][LANES:] for n in gla_in), w["gla_norm"],
                                state_gla[0].reshape(n_dec, hk, GLA_DV))

    kmeta = jnp.pad(sm["kcat"][:N_META], ((0, LANES - N_META), (0, 0)))
    olat_p = _mla_prompt(pr["qcat"], pr["kcat"], kmeta, n_b, 512)
    qd = jnp.pad(jnp.transpose(sm["qcat"][:, LANES:], (1, 0, 2)), ((0, 0), (0, Q_ROWS - MLA_HEADS), (0, 0)))
    cache_kpe_t = jnp.swapaxes(cache_kpe, 2, 3).reshape(-1, MLA_ROPE, PAGE_SIZE)
    o_dec = _mla_decode(page_table, qd, sm["kcat"][LANES:, None, :], sm["ckv"][LANES:, None, :],
                        cache_ckv.reshape(-1, PAGE_SIZE, KV_LORA), cache_kpe_t, 32)
    olat_s = jnp.transpose(o_dec[:, :MLA_HEADS], (1, 0, 2)).astype(BF16)

    y_prompt = _ffn(xp, og_p, olat_p, w, 512).reshape(n_b, seq, d)
    y_sample = _ffn(xs, og_s, olat_s, w, LANES).reshape(n_dec, 1, d)

    bcast = lambda a: jnp.broadcast_to(a[None, :N_META], (n_b, N_META, a.shape[-1]))
    ckv_prompt = jnp.concatenate([bcast(sm["ckv"]), pr["ckv"].reshape(n_b, seq, KV_LORA)], axis=1)[None]
    kpe_prompt = jnp.concatenate([bcast(sm["kpe"]), pr["kpe"].reshape(n_b, seq, MLA_ROPE)], axis=1)[None]
    gla_prompt = s_prompt.reshape(1, n_b, GLA_HEADS, GLA_DK, GLA_DV)
    ckv_sample = sm["ckv"][LANES:].reshape(1, n_dec, 1, KV_LORA)
    kpe_sample = sm["kpe"][LANES:].reshape(1, n_dec, 1, MLA_ROPE)
    gla_sample = s_sample.reshape(1, n_dec, GLA_HEADS, GLA_DK, GLA_DV)
    return (y_prompt, y_sample, ckv_prompt, kpe_prompt, gla_prompt, ckv_sample, kpe_sample, gla_sample)
```

```python
import functools

import jax
import jax.numpy as jnp
from jax import lax
from jax.experimental import pallas as pl
from jax.experimental.pallas import tpu as pltpu

F32 = jnp.float32
BF16 = jnp.bfloat16

D_MODEL = 1024
N_META = 16
EPS = 1e-6
GLA_HEADS = 4
GLA_DV = 128
GLA_DK = 64
GLA_GATE_RANK = 16
GLA_GATE_NORM = 16.0
MLA_HEADS = 4
MLA_NOPE = 128
MLA_ROPE = 64
MLA_V = 128
Q_LORA = 384
KV_LORA = 256
ROPE_THETA = 10000.0
MLA_SCALE = (MLA_NOPE + MLA_ROPE) ** -0.5
PAGE_SIZE = 128
D_FF = 2816

LANES = 128
GLA_STEP = 16
GLA_CHUNK = 64
GLA_SAFE_LOG_DECAY = 40.0
QK_WIDTH = KV_LORA + LANES
NEG_BIG = -1e30
Q_PRESCALE = MLA_SCALE * 1.4426950408889634

_OFF_Q, _OFF_K, _OFF_V, _OFF_G, _OFF_CQ, _OFF_CKV, _OFF_TAIL, _IN_PACKED = (
    0, 256, 512, 1024, 1536, 1920, 2176, 2304)
_TAIL_GR = MLA_ROPE


def _rms(x, w):
    return x * lax.rsqrt(jnp.mean(x * x, axis=-1, keepdims=True) + EPS) * w


def _rope_tile(x, c, s):
    lane = lax.broadcasted_iota(jnp.int32, x.shape, 1)
    from_right = pltpu.roll(x, LANES - MLA_ROPE // 2, 1)
    from_left = pltpu.roll(x, MLA_ROPE // 2, 1)
    swapped = jnp.where(lane < MLA_ROPE // 2, from_right, from_left)
    return x * c + swapped * s


def _dot(a, b):
    return jnp.dot(a, b, preferred_element_type=F32)


def _dot_nt(a, b):
    return lax.dot_general(a, b, (((1,), (1,)), ((), ())), preferred_element_type=F32)


def _proj_kernel(x_ref, cos_ref, sin_ref, nmix_ref, win_ref, wgk_ref, bgk_ref, qn_ref, kvn_ref,
                 wuq_ref, wukt_ref,
                 gq_ref, gk_ref, gv_ref, lg_ref, g_ref, qcat_ref, kcat_ref, ckv_ref, kpe_ref):
    n = _rms(x_ref[...], nmix_ref[...]).astype(BF16)
    proj = _dot(n, win_ref[...])
    gq_ref[...] = proj[:, _OFF_Q:_OFF_K] * (GLA_DK ** -0.5)
    gk_ref[...] = proj[:, _OFF_K:_OFF_V]
    gv_ref[...] = proj[:, _OFF_V:_OFF_G]
    g_ref[...] = proj[:, _OFF_G:_OFF_CQ]
    tail = proj[:, _OFF_TAIL:_IN_PACKED]
    z = _dot(tail.astype(BF16), wgk_ref[...]) + bgk_ref[...]
    lg_ref[...] = jax.nn.log_sigmoid(z) * (1.0 / GLA_GATE_NORM)

    cos = cos_ref[...]
    sin = sin_ref[...]
    cq = _rms(proj[:, _OFF_CQ:_OFF_CKV], qn_ref[...]).astype(BF16)
    qh = _dot(cq, wuq_ref[...])
    for h in range(MLA_HEADS):
        q_nope = qh[:, h * MLA_NOPE:(h + 1) * MLA_NOPE].astype(BF16)
        q_lat = _dot(q_nope, wukt_ref[h])
        base = MLA_HEADS * MLA_NOPE + h * LANES
        q_pe = _rope_tile(qh[:, base:base + LANES], cos, sin)
        qcat_ref[h] = (jnp.concatenate([q_lat, q_pe], axis=-1) * Q_PRESCALE).astype(BF16)

    ckv = _rms(proj[:, _OFF_CKV:_OFF_TAIL], kvn_ref[...])
    ckv_ref[...] = ckv
    k_pe = _rope_tile(tail, cos, sin)
    kpe_ref[...] = k_pe[:, :MLA_ROPE]
    kcat_ref[...] = jnp.concatenate([ckv, k_pe], axis=-1).astype(BF16)


def _full_spec(a):
    nd = a.ndim
    return pl.BlockSpec(a.shape, lambda *_: (0,) * nd)


def _project(x, cos, sin, w, tm):
    t = x.shape[0]
    n_pos_blocks = cos.shape[0] // tm
    row = lambda width: pl.BlockSpec((tm, width), lambda i: (i, 0))
    pos = pl.BlockSpec((tm, LANES), lambda i: (i % n_pos_blocks, 0))
    weights = (w["norm_mix"], w["w_in"], w["w_gk"], w["b_gk"], w["q_norm"], w["kv_norm"],
               w["w_uq"], w["w_ukt"])
    out_shape = (
        jax.ShapeDtypeStruct((t, 256), F32), jax.ShapeDtypeStruct((t, 256), F32),
        jax.ShapeDtypeStruct((t, 512), F32), jax.ShapeDtypeStruct((t, 256), F32),
        jax.ShapeDtypeStruct((t, 512), F32),
        jax.ShapeDtypeStruct((MLA_HEADS, t, QK_WIDTH), BF16),
        jax.ShapeDtypeStruct((t, QK_WIDTH), BF16),
        jax.ShapeDtypeStruct((t, KV_LORA), F32), jax.ShapeDtypeStruct((t, MLA_ROPE), F32))
    out_specs = (row(256), row(256), row(512), row(256), row(512),
                 pl.BlockSpec((MLA_HEADS, tm, QK_WIDTH), lambda i: (0, i, 0)),
                 row(QK_WIDTH), row(KV_LORA), row(MLA_ROPE))
    names = ("gq", "gk", "gv", "lg", "g", "qcat", "kcat", "ckv", "kpe")
    outs = pl.pallas_call(
        _proj_kernel,
        grid=(t // tm,),
        in_specs=[row(D_MODEL), pos, pos] + [_full_spec(a) for a in weights],
        out_specs=out_specs,
        out_shape=out_shape,
        compiler_params=pltpu.CompilerParams(dimension_semantics=("parallel",)),
        name="project",
    )(x, cos, sin, *weights)
    return dict(zip(names, outs))


def _cum_log_decay(lg):
    n = lg.shape[0]
    tri = (lax.broadcasted_iota(jnp.int32, (n, n), 0) >= lax.broadcasted_iota(jnp.int32, (n, n), 1)).astype(F32)
    return jnp.dot(tri, lg, precision=lax.Precision.HIGHEST, preferred_element_type=F32)


def _head_masked_rows(x, pad_rows=0):
    lane_head = lax.broadcasted_iota(jnp.int32, x.shape, 1) // GLA_DK
    blocks = [jnp.where(lane_head == h, x, 0.0) for h in range(GLA_HEADS)]
    if pad_rows:
        blocks.append(jnp.zeros((pad_rows, x.shape[1]), x.dtype))
    return jnp.concatenate(blocks, axis=0)


def _head_stacked_values(v, pad_rows=0):
    blocks = [v[:, h * GLA_DV:(h + 1) * GLA_DV] for h in range(GLA_HEADS)]
    if pad_rows:
        blocks.append(jnp.zeros((pad_rows, GLA_DV), v.dtype))
    return jnp.concatenate(blocks, axis=0)


def _gla_state_update(state, kt, v, b_last, pad_rows=0):
    kbd = _head_masked_rows(kt, pad_rows)
    upd = _dot(kbd.T.astype(BF16), _head_stacked_values(v, pad_rows).astype(BF16))
    dec = jnp.broadcast_to(jnp.exp(b_last), (LANES, kt.shape[1])).T
    return dec * state + upd


def _gla_gate_out(o_heads, g, gn, dtype=BF16):
    normed = jnp.concatenate([_rms(o, gn) for o in o_heads], axis=-1)
    return (normed * (g * jax.nn.sigmoid(g))).astype(dtype)


def _gla_exact_step(state, q, k, lg, v, n_src=None):
    c = q.shape[0]
    n_src = c if n_src is None else n_src
    half = lax.broadcasted_iota(jnp.int32, (c, LANES), 1) // GLA_DK
    row_t = lax.broadcasted_iota(jnp.int32, q.shape, 0)
    b = _cum_log_decay(lg)
    b_last = b[c - 1:c, :]

    o_st = _dot(_head_masked_rows(q * jnp.exp(b)).astype(BF16), state.astype(BF16))
    o_heads = [o_st[h * c:(h + 1) * c] for h in range(GLA_HEADS)]

    for s in range(n_src):
        decay = jnp.exp(jnp.minimum(b - b[s:s + 1, :], 0.0))
        w = jnp.where(row_t >= s, q * k[s:s + 1, :] * decay, 0.0)
        for h in range(GLA_HEADS):
            tile = w[:, (h // 2) * LANES:(h // 2 + 1) * LANES]
            a = jnp.sum(jnp.where(half == h % 2, tile, 0.0), axis=1, keepdims=True)
            o_heads[h] = o_heads[h] + a * v[s:s + 1, h * GLA_DV:(h + 1) * GLA_DV]

    state = _gla_state_update(state, k * jnp.exp(b_last - b), v, b_last, LANES - GLA_HEADS * c)
    return state, o_heads


def _gla_matmul_chunk(state, q, k, b, v):
    c = GLA_CHUNK
    b_last = b[c - 1:c, :]
    qm = _head_masked_rows(q * jnp.exp(b)).astype(BF16)
    k_inv = (k * jnp.exp(-b)).astype(BF16)
    a = _dot_nt(qm, k_inv)
    row_t = lax.broadcasted_iota(jnp.int32, a.shape, 0) % c
    col_s = lax.broadcasted_iota(jnp.int32, a.shape, 1)
    a = jnp.where(col_s <= row_t, a, 0.0).astype(BF16)
    o_st = _dot(qm, state.astype(BF16))
    vb = v.astype(BF16)
    o_heads = [o_st[h * c:(h + 1) * c] + _dot(a[h * c:(h + 1) * c], vb[:, h * GLA_DV:(h + 1) * GLA_DV])
               for h in range(GLA_HEADS)]
    state = _gla_state_update(state, k * jnp.exp(b_last - b), v, b_last)
    return state, o_heads


def _gla_kernel(q_ref, k_ref, lg_ref, v_ref, g_ref, gn_ref, s0_ref, og_ref, sout_ref, s_scr, *, tt):
    j = pl.program_id(1)

    @pl.when(j == 0)
    def _():
        s_scr[...] = s0_ref[0]

    gn = gn_ref[...]

    def exact_steps(start, n_steps):
        state = s_scr[...]
        for u in range(n_steps):
            sl = slice(start + u * GLA_STEP, start + (u + 1) * GLA_STEP)
            state, o_heads = _gla_exact_step(state, q_ref[sl, :], k_ref[sl, :], lg_ref[sl, :], v_ref[sl, :])
            og_ref[sl, :] = _gla_gate_out(o_heads, g_ref[sl, :], gn)
        s_scr[...] = state

    if tt % GLA_CHUNK == 0:
        chunks = [slice(ci * GLA_CHUNK, (ci + 1) * GLA_CHUNK) for ci in range(tt // GLA_CHUNK)]
        bs = [_cum_log_decay(lg_ref[sl, :]) for sl in chunks]
        worst = bs[0][GLA_CHUNK - 1:GLA_CHUNK, :]
        for b in bs[1:]:
            worst = jnp.minimum(worst, b[GLA_CHUNK - 1:GLA_CHUNK, :])
        mild = jnp.max(-worst) <= GLA_SAFE_LOG_DECAY

        @pl.when(mild)
        def _():
            state = s_scr[...]
            for sl, b in zip(chunks, bs):
                state, o_heads = _gla_matmul_chunk(state, q_ref[sl, :], k_ref[sl, :], b, v_ref[sl, :])
                og_ref[sl, :] = _gla_gate_out(o_heads, g_ref[sl, :], gn)
            s_scr[...] = state

        @pl.when(jnp.logical_not(mild))
        def _():
            exact_steps(0, tt // GLA_STEP)
    else:
        exact_steps(0, tt // GLA_STEP)

    @pl.when(j == pl.num_programs(1) - 1)
    def _():
        sout_ref[0] = s_scr[...]


def _gla_scan(q, k, lg, v, g, gla_norm, s0, n_seq, tt):
    t = q.shape[0]
    n_tiles = t // n_seq // tt
    row = lambda width: pl.BlockSpec((tt, width), lambda b, j: (b * n_tiles + j, 0))
    s0_map = (lambda b, j: (b, 0, 0)) if s0.shape[0] == n_seq else (lambda b, j: (0, 0, 0))
    hk = GLA_HEADS * GLA_DK
    og, s_out = pl.pallas_call(
        functools.partial(_gla_kernel, tt=tt),
        grid=(n_seq, n_tiles),
        in_specs=[row(256), row(256), row(256), row(512), row(512), _full_spec(gla_norm),
                  pl.BlockSpec((1, hk, GLA_DV), s0_map)],
        out_specs=(row(512), pl.BlockSpec((1, hk, GLA_DV), lambda b, j: (b, 0, 0))),
        out_shape=(jax.ShapeDtypeStruct((t, 512), BF16),
                   jax.ShapeDtypeStruct((n_seq, hk, GLA_DV), F32)),
        scratch_shapes=[pltpu.VMEM((hk, GLA_DV), F32)],
        compiler_params=pltpu.CompilerParams(dimension_semantics=("parallel", "arbitrary")),
        name="gla_scan",
    )(q, k, lg, v, g, gla_norm, s0)
    return og, s_out


TOKEN_ROWS = 8
TOKEN_SEQS = 8


def _gla_token_kernel(q_ref, k_ref, lg_ref, v_ref, g_ref, gn_ref, s0_ref, og_ref, sout_ref):
    gn = gn_ref[...]
    first = lax.broadcasted_iota(jnp.int32, (TOKEN_ROWS, 1), 0) == 0
    tile = lambda ref, i: jnp.where(first, ref[i:i + 1, :], 0.0)
    rows = []
    for i in range(TOKEN_SEQS):
        state, o_heads = _gla_exact_step(s0_ref[i], tile(q_ref, i), tile(k_ref, i), tile(lg_ref, i),
                                         tile(v_ref, i), n_src=1)
        rows.append(_gla_gate_out(o_heads, tile(g_ref, i), gn, F32)[0:1])
        sout_ref[i] = state
    og_ref[...] = jnp.concatenate(rows, axis=0)


def _gla_token(q, k, lg, v, g, gla_norm, s0):
    n = q.shape[0]
    hk = GLA_HEADS * GLA_DK
    row = lambda width: pl.BlockSpec((TOKEN_SEQS, width), lambda i: (i, 0))
    st = pl.BlockSpec((TOKEN_SEQS, hk, GLA_DV), lambda i: (i, 0, 0))
    return pl.pallas_call(
        _gla_token_kernel,
        grid=(n // TOKEN_SEQS,),
        in_specs=[row(256), row(256), row(256), row(512), row(512), _full_spec(gla_norm), st],
        out_specs=(row(512), st),
        out_shape=(jax.ShapeDtypeStruct((n, 512), F32),
                   jax.ShapeDtypeStruct((n, hk, GLA_DV), F32)),
        compiler_params=pltpu.CompilerParams(dimension_semantics=("parallel",)),
        name="gla_token",
    )(q, k, lg, v, g, gla_norm, s0)


def _mla_prompt_kernel(q_ref, k_ref, km_ref, o_ref, s_scr, sm_scr, mx_scr, l_scr, acc_scr, *, tq):
    qi = pl.program_id(1)
    m_rows = MLA_HEADS * tq
    n_lane_tiles = tq // LANES
    q = q_ref[...].reshape(m_rows, QK_WIDTH)
    lane_fold = lambda a, op: functools.reduce(
        op, [a[:, i * LANES:(i + 1) * LANES] for i in range(a.shape[1] // LANES)])
    wide = lambda a, n: jnp.concatenate([a] * n, axis=-1)
    key_tile = lambda kt: k_ref[pl.ds(pl.multiple_of(kt * tq, tq), tq), :]

    km = km_ref[...]
    s = _dot_nt(q, km)
    col = lax.broadcasted_iota(jnp.int32, s.shape, 1)
    s = jnp.where(col < N_META, s, NEG_BIG)
    sm_scr[...] = s
    mx_scr[...] = s

    def score_tile(kt, masked):
        s = _dot_nt(q, key_tile(kt))
        if masked:
            row = lax.broadcasted_iota(jnp.int32, s.shape, 0) % tq
            col = lax.broadcasted_iota(jnp.int32, s.shape, 1)
            s = jnp.where(col <= row, s, NEG_BIG)
        s_scr[kt] = s
        mx_scr[...] = jnp.maximum(mx_scr[...], lane_fold(s, jnp.maximum))

    def score_body(kt, carry):
        score_tile(kt, False)
        return carry

    lax.fori_loop(0, qi, score_body, 0)
    score_tile(qi, True)
    m = jnp.broadcast_to(jnp.max(mx_scr[...], axis=-1, keepdims=True), (m_rows, LANES))

    p = jnp.exp2(sm_scr[...] - m)
    l_scr[...] = p
    acc_scr[...] = _dot(p.astype(BF16), km[:, :KV_LORA])
    m_wide = wide(m, n_lane_tiles)

    def value_body(kt, carry):
        p = jnp.exp2(s_scr[kt] - m_wide)
        l_scr[...] = l_scr[...] + lane_fold(p, jnp.add)
        acc_scr[...] = acc_scr[...] + _dot(p.astype(BF16), key_tile(kt)[:, :KV_LORA])
        return carry

    lax.fori_loop(0, qi + 1, value_body, 0)
    inv_l = 1.0 / jnp.broadcast_to(jnp.sum(l_scr[...], axis=-1, keepdims=True), (m_rows, LANES))
    o = acc_scr[...] * wide(inv_l, KV_LORA // LANES)
    o_ref[...] = o.reshape(MLA_HEADS, tq, KV_LORA).astype(BF16)


def _mla_prompt(qcat, kcat, kmeta, n_seq, tq):
    t = kcat.shape[0]
    seq = t // n_seq
    nq = seq // tq
    m_rows = MLA_HEADS * tq
    return pl.pallas_call(
        functools.partial(_mla_prompt_kernel, tq=tq),
        grid=(n_seq, nq),
        in_specs=[pl.BlockSpec((MLA_HEADS, tq, QK_WIDTH), lambda b, i: (0, b * nq + i, 0)),
                  pl.BlockSpec((seq, QK_WIDTH), lambda b, i: (b, 0)),
                  _full_spec(kmeta)],
        out_specs=pl.BlockSpec((MLA_HEADS, tq, KV_LORA), lambda b, i: (0, b * nq + i, 0)),
        out_shape=jax.ShapeDtypeStruct((MLA_HEADS, t, KV_LORA), BF16),
        scratch_shapes=[pltpu.VMEM((nq, m_rows, tq), F32),
                        pltpu.VMEM((m_rows, LANES), F32),
                        pltpu.VMEM((m_rows, LANES), F32),
                        pltpu.VMEM((m_rows, LANES), F32),
                        pltpu.VMEM((m_rows, KV_LORA), F32)],
        compiler_params=pltpu.CompilerParams(dimension_semantics=("parallel", "arbitrary")),
        name="mla_prompt",
    )(qcat, kcat, kmeta)


Q_ROWS = 8
DEC_GROUP = 8


def _mla_decode_kernel(pt_ref, q_ref, kn_ref, cn_ref, ckv_hbm, kpe_hbm, o_ref,
                       ckv_buf, kpe_buf, sem, *, pg, nc, n_seq):
    s_id = pl.program_id(0)

    def start_chunk(seq_i, ci):
        for i in range(pg):
            page = pt_ref[seq_i, ci * pg + i]
            pltpu.make_async_copy(ckv_hbm.at[page], ckv_buf.at[ci, i], sem.at[0, ci]).start()
            pltpu.make_async_copy(kpe_hbm.at[page], kpe_buf.at[ci, i], sem.at[1, ci]).start()

    def wait_chunk(ci):
        pltpu.make_async_copy(ckv_hbm.at[pl.ds(0, pg)], ckv_buf.at[ci], sem.at[0, ci]).wait()
        pltpu.make_async_copy(kpe_hbm.at[pl.ds(0, pg)], kpe_buf.at[ci], sem.at[1, ci]).wait()

    @pl.when(s_id == 0)
    def _():
        for ci in range(nc):
            start_chunk(0, ci)

    nxt = jnp.minimum(s_id + 1, n_seq - 1)

    q = q_ref[0]
    q_pe = q[:, KV_LORA:KV_LORA + MLA_ROPE]
    q_rows = jnp.concatenate([q[:, :KV_LORA].astype(F32), jnp.zeros((LANES - Q_ROWS, KV_LORA), F32)], axis=0)
    q_t = q_rows.T.astype(BF16)
    group_keys = DEC_GROUP * PAGE_SIZE
    n_groups = pg // DEC_GROUP
    m = jnp.full((Q_ROWS, 1), NEG_BIG, F32)
    l = jnp.zeros((Q_ROWS, 1), F32)
    acc = jnp.zeros((Q_ROWS, KV_LORA), F32)
    for ci in range(nc):
        wait_chunk(ci)
        ckv_groups, parts = [], []
        for g in range(n_groups):
            ckv_g = ckv_buf[ci, g * DEC_GROUP:(g + 1) * DEC_GROUP].reshape(group_keys, KV_LORA).astype(BF16)
            ckv_groups.append(ckv_g)
            s_lat_t = _dot(ckv_g, q_t)
            for i in range(DEC_GROUP):
                s_lat = s_lat_t[i * PAGE_SIZE:(i + 1) * PAGE_SIZE].T[:Q_ROWS]
                kpe_t = kpe_buf[ci, g * DEC_GROUP + i].astype(BF16)
                parts.append(s_lat + _dot(q_pe, kpe_t))
        s = jnp.concatenate(parts, axis=-1)
        m_new = jnp.maximum(m, jnp.max(s, axis=-1, keepdims=True))
        corr = jnp.exp2(m - m_new)
        pb = jnp.exp2(s - m_new)
        l = l * corr + jnp.sum(pb, axis=-1, keepdims=True)
        pb = pb.astype(BF16)
        pv = _dot(pb[:, :group_keys], ckv_groups[0])
        for g in range(1, n_groups):
            pv = pv + _dot(pb[:, g * group_keys:(g + 1) * group_keys], ckv_groups[g])
        acc = acc * corr + pv
        m = m_new
        start_chunk(nxt, ci)

    s_self = jnp.sum(q.astype(F32) * kn_ref[0].astype(F32), axis=-1, keepdims=True)
    m_fin = jnp.maximum(m, s_self)
    corr = jnp.exp2(m - m_fin)
    p_self = jnp.exp2(s_self - m_fin)
    l_fin = l * corr + p_self
    o_ref[0] = (acc * corr + p_self * cn_ref[0]) * (1.0 / l_fin)

    @pl.when(s_id == n_seq - 1)
    def _():
        for ci in range(nc):
            wait_chunk(ci)


def _mla_decode(page_table, qd, kn, cn, cache_ckv, cache_kpe_t, pg):
    n_seq, n_pages = page_table.shape
    nc = n_pages // pg
    grid_spec = pltpu.PrefetchScalarGridSpec(
        num_scalar_prefetch=1,
        grid=(n_seq,),
        in_specs=[pl.BlockSpec((1, Q_ROWS, QK_WIDTH), lambda s, pt: (s, 0, 0)),
                  pl.BlockSpec((1, 1, QK_WIDTH), lambda s, pt: (s, 0, 0)),
                  pl.BlockSpec((1, 1, KV_LORA), lambda s, pt: (s, 0, 0)),
                  pl.BlockSpec(memory_space=pl.ANY),
                  pl.BlockSpec(memory_space=pl.ANY)],
        out_specs=pl.BlockSpec((1, Q_ROWS, KV_LORA), lambda s, pt: (s, 0, 0)),
        scratch_shapes=[pltpu.VMEM((nc, pg, PAGE_SIZE, KV_LORA), F32),
                        pltpu.VMEM((nc, pg, MLA_ROPE, PAGE_SIZE), F32),
                        pltpu.SemaphoreType.DMA((2, nc))])
    return pl.pallas_call(
        functools.partial(_mla_decode_kernel, pg=pg, nc=nc, n_seq=n_seq),
        grid_spec=grid_spec,
        out_shape=jax.ShapeDtypeStruct((n_seq, Q_ROWS, KV_LORA), F32),
        compiler_params=pltpu.CompilerParams(dimension_semantics=("arbitrary",)),
        name="mla_decode",
    )(page_table, qd, kn, cn, cache_ckv, cache_kpe_t)


def _ffn_kernel(x_ref, og_ref, ol_ref, wuv_ref, wout_ref, nffn_ref, wg_ref, wu_ref, wd_ref, nfin_ref, y_ref):
    om = [_dot(ol_ref[h], wuv_ref[h]).astype(BF16) for h in range(MLA_HEADS)]
    cat = jnp.concatenate([og_ref[...].astype(BF16)] + om, axis=-1)
    h1 = x_ref[...] + _dot(cat, wout_ref[...])
    n = _rms(h1, nffn_ref[...]).astype(BF16)
    gate = _dot(n, wg_ref[...])
    up = _dot(n, wu_ref[...])
    act = (gate * jax.nn.sigmoid(gate) * up).astype(BF16)
    h2 = h1 + _dot(act, wd_ref[...])
    y_ref[...] = _rms(h2, nfin_ref[...])


def _ffn(x, og, olat, w, tm):
    t = x.shape[0]
    row = lambda width: pl.BlockSpec((tm, width), lambda i: (i, 0))
    weights = (w["w_uv"], w["w_out"], w["norm_ffn"], w["w_gate"], w["w_up"], w["w_down"], w["norm_final"])
    return pl.pallas_call(
        _ffn_kernel,
        grid=(t // tm,),
        in_specs=[row(D_MODEL), row(512), pl.BlockSpec((MLA_HEADS, tm, KV_LORA), lambda i: (0, i, 0))]
                 + [_full_spec(a) for a in weights],
        out_specs=row(D_MODEL),
        out_shape=jax.ShapeDtypeStruct((t, D_MODEL), F32),
        compiler_params=pltpu.CompilerParams(dimension_semantics=("parallel",)),
        name="ffn",
    )(x, og, olat, *weights)


def _prep_weights(norm_mix, w_in, w_gk, b_gk, gla_norm, q_norm, kv_norm, w_uq, w_ukv, w_out,
                  norm_ffn, w_gate, w_up, w_down, norm_final):
    hk = GLA_HEADS * GLA_DK
    hv = GLA_HEADS * GLA_DV
    sizes = (hk, hk, hv, GLA_GATE_RANK, hv, Q_LORA, KV_LORA, MLA_ROPE)
    bounds = [0]
    for sz in sizes:
        bounds.append(bounds[-1] + sz)
    wq, wk, wv, wgr, wg, wcq, wckv, wkpe = (w_in[0][:, bounds[i]:bounds[i + 1]] for i in range(8))
    pad = jnp.zeros((D_MODEL, _IN_PACKED - _OFF_TAIL - MLA_ROPE - GLA_GATE_RANK), w_in.dtype)
    w_in_p = jnp.concatenate([wq, wk, wv, wg, wcq, wckv, wkpe, wgr, pad], axis=1).astype(BF16)

    w_gk_p = jnp.zeros((LANES, hk), F32).at[_TAIL_GR:_TAIL_GR + GLA_GATE_RANK].set(w_gk[0]).astype(BF16)

    wuq = w_uq[0].reshape(Q_LORA, MLA_HEADS, MLA_NOPE + MLA_ROPE)
    wuq_nope = wuq[..., :MLA_NOPE].reshape(Q_LORA, MLA_HEADS * MLA_NOPE)
    wuq_rope = jnp.pad(wuq[..., MLA_NOPE:], ((0, 0), (0, 0), (0, LANES - MLA_ROPE)))
    w_uq_p = jnp.concatenate([wuq_nope, wuq_rope.reshape(Q_LORA, MLA_HEADS * LANES)], axis=1).astype(BF16)

    wukv = w_ukv[0].reshape(KV_LORA, MLA_HEADS, MLA_NOPE + MLA_V)
    w_ukt = jnp.transpose(wukv[..., :MLA_NOPE], (1, 2, 0)).astype(BF16)
    w_uv = jnp.transpose(wukv[..., MLA_NOPE:], (1, 0, 2)).astype(BF16)

    r = lambda a: a.reshape(1, -1).astype(F32)
    return dict(norm_mix=r(norm_mix[0]), w_in=w_in_p, w_gk=w_gk_p, b_gk=r(b_gk[0]), q_norm=r(q_norm[0]),
                kv_norm=r(kv_norm[0]), w_uq=w_uq_p, w_ukt=w_ukt, w_uv=w_uv, gla_norm=r(gla_norm[0]),
                w_out=w_out[0].astype(BF16), norm_ffn=r(norm_ffn[0]), w_gate=w_gate[0].astype(BF16),
                w_up=w_up[0].astype(BF16), w_down=w_down[0].astype(BF16), norm_final=r(norm_final))


def _rope_tables(pos):
    half = MLA_ROPE // 2
    inv = ROPE_THETA ** (-jnp.arange(half, dtype=F32) / half)
    ang = pos.astype(F32)[:, None] * inv[None, :]
    cos, sin = jnp.cos(ang), jnp.sin(ang)
    z = jnp.zeros((pos.shape[0], LANES - MLA_ROPE), F32)
    return jnp.concatenate([cos, cos, z], axis=1), jnp.concatenate([-sin, sin, z], axis=1)


def kernel(x_prompt, x_sample, cache_ckv, cache_kpe, state_gla, page_table, meta_tokens, norm_mix, w_in, w_gk, b_gk, gla_norm, q_norm, kv_norm, w_uq, w_ukv, w_out, norm_ffn, w_gate, w_up, w_down, norm_final):
    n_b, seq, d = x_prompt.shape
    n_dec, t_dec, _ = x_sample.shape
    assert w_in.shape[0] == 1 and t_dec == 1 and d == D_MODEL
    n_pages = page_table.shape[1]
    past = n_pages * PAGE_SIZE
    hk = GLA_HEADS * GLA_DK
    w = _prep_weights(norm_mix, w_in, w_gk, b_gk, gla_norm, q_norm, kv_norm, w_uq, w_ukv, w_out,
                      norm_ffn, w_gate, w_up, w_down, norm_final)

    small_rows = 2 * LANES
    assert N_META <= LANES and n_dec == LANES
    xs = x_sample[:, 0]
    x_small = jnp.concatenate([meta_tokens.astype(F32), jnp.zeros((LANES - N_META, d), F32), xs], axis=0)
    pos_small = jnp.concatenate([jnp.arange(N_META), jnp.zeros((LANES - N_META,), jnp.int32),
                                 jnp.full((n_dec,), past, jnp.int32)])
    sm = _project(x_small, *_rope_tables(pos_small), w, small_rows)
    xp = x_prompt.reshape(n_b * seq, d)
    pr = _project(xp, *_rope_tables(N_META + jnp.arange(seq)), w, 512)

    gla_in = ("gq", "gk", "lg", "gv", "g")
    _, s_meta = _gla_scan(*(sm[n][:N_META] for n in gla_in), w["gla_norm"],
                          jnp.zeros((1, hk, GLA_DV), F32), 1, GLA_STEP)
    og_p, s_prompt = _gla_scan(*(pr[n] for n in gla_in), w["gla_norm"], s_meta, n_b, 512)
    og_s, s_sample = _gla_token(*(sm[n][LANES:] for n in gla_in), w["gla_norm"],
                                state_gla[0].reshape(n_dec, hk, GLA_DV))

    kmeta = jnp.pad(sm["kcat"][:N_META], ((0, LANES - N_META), (0, 0)))
    olat_p = _mla_prompt(pr["qcat"], pr["kcat"], kmeta, n_b, 512)
    qd = jnp.pad(jnp.transpose(sm["qcat"][:, LANES:], (1, 0, 2)), ((0, 0), (0, Q_ROWS - MLA_HEADS), (0, 0)))
    cache_kpe_t = jnp.swapaxes(cache_kpe, 2, 3).reshape(-1, MLA_ROPE, PAGE_SIZE)
    o_dec = _mla_decode(page_table, qd, sm["kcat"][LANES:, None, :], sm["ckv"][LANES:, None, :],
                        cache_ckv.reshape(-1, PAGE_SIZE, KV_LORA), cache_kpe_t, 32)
    olat_s = jnp.transpose(o_dec[:, :MLA_HEADS], (1, 0, 2)).astype(BF16)

    y_prompt = _ffn(xp, og_p, olat_p, w, 512).reshape(n_b, seq, d)
    y_sample = _ffn(xs, og_s, olat_s, w, LANES).reshape(n_dec, 1, d)

    bcast = lambda a: jnp.broadcast_to(a[None, :N_META], (n_b, N_META, a.shape[-1]))
    ckv_prompt = jnp.concatenate([bcast(sm["ckv"]), pr["ckv"].reshape(n_b, seq, KV_LORA)], axis=1)[None]
    kpe_prompt = jnp.concatenate([bcast(sm["kpe"]), pr["kpe"].reshape(n_b, seq, MLA_ROPE)], axis=1)[None]
    gla_prompt = s_prompt.reshape(1, n_b, GLA_HEADS, GLA_DK, GLA_DV)
    ckv_sample = sm["ckv"][LANES:].reshape(1, n_dec, 1, KV_LORA)
    kpe_sample = sm["kpe"][LANES:].reshape(1, n_dec, 1, MLA_ROPE)
    gla_sample = s_sample.reshape(1, n_dec, GLA_HEADS, GLA_DK, GLA_DV)
    return (y_prompt, y_sample, ckv_prompt, kpe_prompt, gla_prompt, ckv_sample, kpe_sample, gla_sample)
```

```python
import functools

import jax
import jax.numpy as jnp
from jax import lax
from jax.experimental import pallas as pl
from jax.experimental.pallas import tpu as pltpu

F32 = jnp.float32
BF16 = jnp.bfloat16

D_MODEL = 1024
N_META = 16
EPS = 1e-6
GLA_HEADS = 4
GLA_DV = 128
GLA_DK = 64
GLA_GATE_RANK = 16
GLA_GATE_NORM = 16.0
MLA_HEADS = 4
MLA_NOPE = 128
MLA_ROPE = 64
MLA_V = 128
Q_LORA = 384
KV_LORA = 256
ROPE_THETA = 10000.0
MLA_SCALE = (MLA_NOPE + MLA_ROPE) ** -0.5
PAGE_SIZE = 128
D_FF = 2816

LANES = 128
GLA_STEP = 16
GLA_CHUNK = 64
GLA_SAFE_LOG_DECAY = 40.0
QK_WIDTH = KV_LORA + LANES
NEG_BIG = -1e30
Q_PRESCALE = MLA_SCALE * 1.4426950408889634

_OFF_Q, _OFF_K, _OFF_V, _OFF_G, _OFF_CQ, _OFF_CKV, _OFF_TAIL, _IN_PACKED = (
    0, 256, 512, 1024, 1536, 1920, 2176, 2304)
_TAIL_GR = MLA_ROPE


def _rms(x, w):
    return x * lax.rsqrt(jnp.mean(x * x, axis=-1, keepdims=True) + EPS) * w


def _rope_tile(x, c, s):
    lane = lax.broadcasted_iota(jnp.int32, x.shape, 1)
    from_right = pltpu.roll(x, LANES - MLA_ROPE // 2, 1)
    from_left = pltpu.roll(x, MLA_ROPE // 2, 1)
    swapped = jnp.where(lane < MLA_ROPE // 2, from_right, from_left)
    return x * c + swapped * s


def _dot(a, b):
    return jnp.dot(a, b, preferred_element_type=F32)


def _dot_nt(a, b):
    return lax.dot_general(a, b, (((1,), (1,)), ((), ())), preferred_element_type=F32)


def _proj_kernel(x_ref, cos_ref, sin_ref, lead_ckv_ref, lead_kpe_ref, nmix_ref, win_ref, wgk_ref, bgk_ref,
                 qn_ref, kvn_ref, wuq_ref, wukt_ref,
                 gq_ref, gk_ref, gv_ref, lg_ref, g_ref, qcat_ref, kcat_ref, ckv_ref, kpe_ref, *, lead):
    tm = x_ref.shape[0]
    j = pl.program_id(1)
    n = _rms(x_ref[...], nmix_ref[...]).astype(BF16)
    proj = _dot_nt(n, win_ref[...])
    gq_ref[...] = proj[:, _OFF_Q:_OFF_K] * (GLA_DK ** -0.5)
    gk_ref[...] = proj[:, _OFF_K:_OFF_V]
    gv_ref[...] = proj[:, _OFF_V:_OFF_G]
    g_ref[...] = proj[:, _OFF_G:_OFF_CQ]
    tail = proj[:, _OFF_TAIL:_IN_PACKED]
    z = _dot(tail.astype(BF16), wgk_ref[...]) + bgk_ref[...]
    lg_ref[...] = jax.nn.log_sigmoid(z) * (1.0 / GLA_GATE_NORM)

    cos = cos_ref[...]
    sin = sin_ref[...]
    cq = _rms(proj[:, _OFF_CQ:_OFF_CKV], qn_ref[...]).astype(BF16)
    qh = _dot(cq, wuq_ref[...])
    for h in range(MLA_HEADS):
        q_nope = qh[:, h * MLA_NOPE:(h + 1) * MLA_NOPE].astype(BF16)
        q_lat = _dot(q_nope, wukt_ref[h])
        base = MLA_HEADS * MLA_NOPE + h * LANES
        q_pe = _rope_tile(qh[:, base:base + LANES], cos, sin)
        qcat_ref[h] = (jnp.concatenate([q_lat, q_pe], axis=-1) * Q_PRESCALE).astype(BF16)

    ckv = _rms(proj[:, _OFF_CKV:_OFF_TAIL], kvn_ref[...])
    k_pe = _rope_tile(tail, cos, sin)
    kcat_ref[...] = jnp.concatenate([ckv, k_pe], axis=-1).astype(BF16)

    if lead:
        @pl.when(j == 0)
        def _():
            ckv_ref[0, 0:lead, :] = lead_ckv_ref[...]
            kpe_ref[0, 0:lead, :] = lead_kpe_ref[...]
    rows = pl.ds(pl.multiple_of(lead + j * tm, 8), tm)
    ckv_ref[0, rows, :] = ckv
    kpe_ref[0, rows, :] = k_pe[:, :MLA_ROPE]


def _full_spec(a):
    nd = a.ndim
    return pl.BlockSpec(a.shape, lambda *_: (0,) * nd)


def _project(x, cos, sin, w, tm, n_seq=1, lead_ckv=None, lead_kpe=None):
    t = x.shape[0]
    tiles = t // n_seq // tm
    lead = 0 if lead_ckv is None else lead_ckv.shape[0]
    if lead == 0:
        lead_ckv, lead_kpe = jnp.zeros((8, KV_LORA), F32), jnp.zeros((8, MLA_ROPE), F32)
    seq_rows = lead + tiles * tm
    row = lambda width: pl.BlockSpec((tm, width), lambda b, j: (b * tiles + j, 0))
    pos = pl.BlockSpec((tm, LANES), lambda b, j: (j, 0))
    per_seq = lambda width: pl.BlockSpec((1, seq_rows, width), lambda b, j: (b, 0, 0))
    weights = (w["norm_mix"], w["w_in"], w["w_gk"], w["b_gk"], w["q_norm"], w["kv_norm"],
               w["w_uq"], w["w_ukt"])
    out_shape = (
        jax.ShapeDtypeStruct((t, 256), F32), jax.ShapeDtypeStruct((t, 256), F32),
        jax.ShapeDtypeStruct((t, 512), F32), jax.ShapeDtypeStruct((t, 256), F32),
        jax.ShapeDtypeStruct((t, 512), F32),
        jax.ShapeDtypeStruct((MLA_HEADS, t, QK_WIDTH), BF16),
        jax.ShapeDtypeStruct((t, QK_WIDTH), BF16),
        jax.ShapeDtypeStruct((n_seq, seq_rows, KV_LORA), F32),
        jax.ShapeDtypeStruct((n_seq, seq_rows, MLA_ROPE), F32))
    out_specs = (row(256), row(256), row(512), row(256), row(512),
                 pl.BlockSpec((MLA_HEADS, tm, QK_WIDTH), lambda b, j: (0, b * tiles + j, 0)),
                 row(QK_WIDTH), per_seq(KV_LORA), per_seq(MLA_ROPE))
    names = ("gq", "gk", "gv", "lg", "g", "qcat", "kcat", "ckv", "kpe")
    outs = pl.pallas_call(
        functools.partial(_proj_kernel, lead=lead),
        grid=(n_seq, tiles),
        in_specs=[row(D_MODEL), pos, pos, _full_spec(lead_ckv), _full_spec(lead_kpe)]
                 + [_full_spec(a) for a in weights],
        out_specs=out_specs,
        out_shape=out_shape,
        compiler_params=pltpu.CompilerParams(dimension_semantics=("parallel", "arbitrary")),
        name="project",
    )(x, cos, sin, lead_ckv, lead_kpe, *weights)
    return dict(zip(names, outs))


def _cum_log_decay(lg):
    n = lg.shape[0]
    tri = (lax.broadcasted_iota(jnp.int32, (n, n), 0) >= lax.broadcasted_iota(jnp.int32, (n, n), 1)).astype(F32)
    return jnp.dot(tri, lg, precision=lax.Precision.HIGHEST, preferred_element_type=F32)


def _head_masked_rows(x, pad_rows=0):
    lane_head = lax.broadcasted_iota(jnp.int32, x.shape, 1) // GLA_DK
    blocks = [jnp.where(lane_head == h, x, 0.0) for h in range(GLA_HEADS)]
    if pad_rows:
        blocks.append(jnp.zeros((pad_rows, x.shape[1]), x.dtype))
    return jnp.concatenate(blocks, axis=0)


def _head_stacked_values(v, pad_rows=0):
    blocks = [v[:, h * GLA_DV:(h + 1) * GLA_DV] for h in range(GLA_HEADS)]
    if pad_rows:
        blocks.append(jnp.zeros((pad_rows, GLA_DV), v.dtype))
    return jnp.concatenate(blocks, axis=0)


def _gla_state_update(state, kt, v, b_last, pad_rows=0):
    kbd = _head_masked_rows(kt, pad_rows)
    upd = _dot(kbd.T.astype(BF16), _head_stacked_values(v, pad_rows).astype(BF16))
    dec = jnp.broadcast_to(jnp.exp(b_last), (LANES, kt.shape[1])).T
    return dec * state + upd


def _gla_gate_out(o_heads, g, gn, dtype=BF16):
    normed = jnp.concatenate([_rms(o, gn) for o in o_heads], axis=-1)
    return (normed * (g * jax.nn.sigmoid(g))).astype(dtype)


def _gla_exact_step(state, q, k, lg, v, n_src=None):
    c = q.shape[0]
    n_src = c if n_src is None else n_src
    half = lax.broadcasted_iota(jnp.int32, (c, LANES), 1) // GLA_DK
    row_t = lax.broadcasted_iota(jnp.int32, q.shape, 0)
    b = _cum_log_decay(lg)
    b_last = b[c - 1:c, :]

    o_st = _dot(_head_masked_rows(q * jnp.exp(b)).astype(BF16), state.astype(BF16))
    o_heads = [o_st[h * c:(h + 1) * c] for h in range(GLA_HEADS)]

    for s in range(n_src):
        decay = jnp.exp(jnp.minimum(b - b[s:s + 1, :], 0.0))
        w = jnp.where(row_t >= s, q * k[s:s + 1, :] * decay, 0.0)
        for h in range(GLA_HEADS):
            tile = w[:, (h // 2) * LANES:(h // 2 + 1) * LANES]
            a = jnp.sum(jnp.where(half == h % 2, tile, 0.0), axis=1, keepdims=True)
            o_heads[h] = o_heads[h] + a * v[s:s + 1, h * GLA_DV:(h + 1) * GLA_DV]

    state = _gla_state_update(state, k * jnp.exp(b_last - b), v, b_last, LANES - GLA_HEADS * c)
    return state, o_heads


def _gla_matmul_chunk(state, q, k, b, v):
    c = GLA_CHUNK
    b_last = b[c - 1:c, :]
    qm = _head_masked_rows(q * jnp.exp(b)).astype(BF16)
    k_inv = (k * jnp.exp(-b)).astype(BF16)
    a = _dot_nt(qm, k_inv)
    row_t = lax.broadcasted_iota(jnp.int32, a.shape, 0) % c
    col_s = lax.broadcasted_iota(jnp.int32, a.shape, 1)
    a = jnp.where(col_s <= row_t, a, 0.0).astype(BF16)
    o_st = _dot(qm, state.astype(BF16))
    vb = v.astype(BF16)
    o_heads = [o_st[h * c:(h + 1) * c] + _dot(a[h * c:(h + 1) * c], vb[:, h * GLA_DV:(h + 1) * GLA_DV])
               for h in range(GLA_HEADS)]
    state = _gla_state_update(state, k * jnp.exp(b_last - b), v, b_last)
    return state, o_heads


def _gla_kernel(q_ref, k_ref, lg_ref, v_ref, g_ref, gn_ref, s0_ref, og_ref, sout_ref, s_scr, *, tt):
    j = pl.program_id(1)

    @pl.when(j == 0)
    def _():
        s_scr[...] = s0_ref[0]

    gn = gn_ref[...]

    def exact_steps(start, n_steps):
        state = s_scr[...]
        for u in range(n_steps):
            sl = slice(start + u * GLA_STEP, start + (u + 1) * GLA_STEP)
            state, o_heads = _gla_exact_step(state, q_ref[sl, :], k_ref[sl, :], lg_ref[sl, :], v_ref[sl, :])
            og_ref[sl, :] = _gla_gate_out(o_heads, g_ref[sl, :], gn)
        s_scr[...] = state

    if tt % GLA_CHUNK == 0:
        chunks = [slice(ci * GLA_CHUNK, (ci + 1) * GLA_CHUNK) for ci in range(tt // GLA_CHUNK)]
        bs = [_cum_log_decay(lg_ref[sl, :]) for sl in chunks]
        worst = bs[0][GLA_CHUNK - 1:GLA_CHUNK, :]
        for b in bs[1:]:
            worst = jnp.minimum(worst, b[GLA_CHUNK - 1:GLA_CHUNK, :])
        mild = jnp.max(-worst) <= GLA_SAFE_LOG_DECAY

        @pl.when(mild)
        def _():
            state = s_scr[...]
            for sl, b in zip(chunks, bs):
                state, o_heads = _gla_matmul_chunk(state, q_ref[sl, :], k_ref[sl, :], b, v_ref[sl, :])
                og_ref[sl, :] = _gla_gate_out(o_heads, g_ref[sl, :], gn)
            s_scr[...] = state

        @pl.when(jnp.logical_not(mild))
        def _():
            exact_steps(0, tt // GLA_STEP)
    else:
        exact_steps(0, tt // GLA_STEP)

    @pl.when(j == pl.num_programs(1) - 1)
    def _():
        sout_ref[0] = s_scr[...]


def _gla_scan(q, k, lg, v, g, gla_norm, s0, n_seq, tt):
    t = q.shape[0]
    n_tiles = t // n_seq // tt
    row = lambda width: pl.BlockSpec((tt, width), lambda b, j: (b * n_tiles + j, 0))
    s0_map = (lambda b, j: (b, 0, 0)) if s0.shape[0] == n_seq else (lambda b, j: (0, 0, 0))
    hk = GLA_HEADS * GLA_DK
    og, s_out = pl.pallas_call(
        functools.partial(_gla_kernel, tt=tt),
        grid=(n_seq, n_tiles),
        in_specs=[row(256), row(256), row(256), row(512), row(512), _full_spec(gla_norm),
                  pl.BlockSpec((1, hk, GLA_DV), s0_map)],
        out_specs=(row(512), pl.BlockSpec((1, hk, GLA_DV), lambda b, j: (b, 0, 0))),
        out_shape=(jax.ShapeDtypeStruct((t, 512), BF16),
                   jax.ShapeDtypeStruct((n_seq, hk, GLA_DV), F32)),
        scratch_shapes=[pltpu.VMEM((hk, GLA_DV), F32)],
        compiler_params=pltpu.CompilerParams(dimension_semantics=("parallel", "arbitrary")),
        name="gla_scan",
    )(q, k, lg, v, g, gla_norm, s0)
    return og, s_out


TOKEN_ROWS = 8
TOKEN_SEQS = 8


def _gla_token_kernel(q_ref, k_ref, lg_ref, v_ref, g_ref, gn_ref, s0_ref, og_ref, sout_ref):
    gn = gn_ref[...]
    first = lax.broadcasted_iota(jnp.int32, (TOKEN_ROWS, 1), 0) == 0
    tile = lambda ref, i: jnp.where(first, ref[i:i + 1, :], 0.0)
    rows = []
    for i in range(TOKEN_SEQS):
        state, o_heads = _gla_exact_step(s0_ref[i], tile(q_ref, i), tile(k_ref, i), tile(lg_ref, i),
                                         tile(v_ref, i), n_src=1)
        rows.append(_gla_gate_out(o_heads, tile(g_ref, i), gn, F32)[0:1])
        sout_ref[i] = state
    og_ref[...] = jnp.concatenate(rows, axis=0)


def _gla_token(q, k, lg, v, g, gla_norm, s0):
    n = q.shape[0]
    hk = GLA_HEADS * GLA_DK
    row = lambda width: pl.BlockSpec((TOKEN_SEQS, width), lambda i: (i, 0))
    st = pl.BlockSpec((TOKEN_SEQS, hk, GLA_DV), lambda i: (i, 0, 0))
    return pl.pallas_call(
        _gla_token_kernel,
        grid=(n // TOKEN_SEQS,),
        in_specs=[row(256), row(256), row(256), row(512), row(512), _full_spec(gla_norm), st],
        out_specs=(row(512), st),
        out_shape=(jax.ShapeDtypeStruct((n, 512), F32),
                   jax.ShapeDtypeStruct((n, hk, GLA_DV), F32)),
        compiler_params=pltpu.CompilerParams(dimension_semantics=("parallel",)),
        name="gla_token",
    )(q, k, lg, v, g, gla_norm, s0)


def _mla_prompt_kernel(q_ref, k_ref, km_ref, o_ref, s_scr, sm_scr, mx_scr, l_scr, acc_scr, *, tq):
    qi = pl.program_id(1)
    m_rows = MLA_HEADS * tq
    n_lane_tiles = tq // LANES
    q = q_ref[...].reshape(m_rows, QK_WIDTH)
    lane_fold = lambda a, op: functools.reduce(
        op, [a[:, i * LANES:(i + 1) * LANES] for i in range(a.shape[1] // LANES)])
    wide = lambda a, n: jnp.concatenate([a] * n, axis=-1)
    key_tile = lambda kt: k_ref[pl.ds(pl.multiple_of(kt * tq, tq), tq), :]

    km = km_ref[...]
    s = _dot_nt(q, km)
    col = lax.broadcasted_iota(jnp.int32, s.shape, 1)
    s = jnp.where(col < N_META, s, NEG_BIG)
    sm_scr[...] = s
    mx_scr[...] = s

    def score_tile(kt, masked):
        s = _dot_nt(q, key_tile(kt))
        if masked:
            row = lax.broadcasted_iota(jnp.int32, s.shape, 0) % tq
            col = lax.broadcasted_iota(jnp.int32, s.shape, 1)
            s = jnp.where(col <= row, s, NEG_BIG)
        s_scr[kt] = s
        mx_scr[...] = jnp.maximum(mx_scr[...], lane_fold(s, jnp.maximum))

    def score_body(kt, carry):
        score_tile(kt, False)
        return carry

    lax.fori_loop(0, qi, score_body, 0)
    score_tile(qi, True)
    m = jnp.broadcast_to(jnp.max(mx_scr[...], axis=-1, keepdims=True), (m_rows, LANES))

    p = jnp.exp2(sm_scr[...] - m)
    l_scr[...] = p
    acc_scr[...] = _dot(p.astype(BF16), km[:, :KV_LORA])
    m_wide = wide(m, n_lane_tiles)

    def value_body(kt, carry):
        p = jnp.exp2(s_scr[kt] - m_wide)
        l_scr[...] = l_scr[...] + lane_fold(p, jnp.add)
        acc_scr[...] = acc_scr[...] + _dot(p.astype(BF16), key_tile(kt)[:, :KV_LORA])
        return carry

    lax.fori_loop(0, qi + 1, value_body, 0)
    inv_l = 1.0 / jnp.broadcast_to(jnp.sum(l_scr[...], axis=-1, keepdims=True), (m_rows, LANES))
    o = acc_scr[...] * wide(inv_l, KV_LORA // LANES)
    o_ref[...] = o.reshape(MLA_HEADS, tq, KV_LORA).astype(BF16)


def _mla_prompt(qcat, kcat, kmeta, n_seq, tq):
    t = kcat.shape[0]
    seq = t // n_seq
    nq = seq // tq
    m_rows = MLA_HEADS * tq
    return pl.pallas_call(
        functools.partial(_mla_prompt_kernel, tq=tq),
        grid=(n_seq, nq),
        in_specs=[pl.BlockSpec((MLA_HEADS, tq, QK_WIDTH), lambda b, i: (0, b * nq + i, 0)),
                  pl.BlockSpec((seq, QK_WIDTH), lambda b, i: (b, 0)),
                  _full_spec(kmeta)],
        out_specs=pl.BlockSpec((MLA_HEADS, tq, KV_LORA), lambda b, i: (0, b * nq + i, 0)),
        out_shape=jax.ShapeDtypeStruct((MLA_HEADS, t, KV_LORA), BF16),
        scratch_shapes=[pltpu.VMEM((nq, m_rows, tq), F32),
                        pltpu.VMEM((m_rows, LANES), F32),
                        pltpu.VMEM((m_rows, LANES), F32),
                        pltpu.VMEM((m_rows, LANES), F32),
                        pltpu.VMEM((m_rows, KV_LORA), F32)],
        compiler_params=pltpu.CompilerParams(dimension_semantics=("parallel", "arbitrary")),
        name="mla_prompt",
    )(qcat, kcat, kmeta)


Q_ROWS = 8
DEC_GROUP = 8


def _mla_decode_kernel(pt_ref, q_ref, kn_ref, cn_ref, ckv_hbm, kpe_hbm, o_ref,
                       ckv_buf, kpe_buf, sem, *, pg, nc, n_seq):
    s_id = pl.program_id(0)

    def start_chunk(seq_i, ci):
        for i in range(pg):
            page = pt_ref[seq_i, ci * pg + i]
            pltpu.make_async_copy(ckv_hbm.at[page], ckv_buf.at[ci, i], sem.at[0, ci]).start()
            pltpu.make_async_copy(kpe_hbm.at[page], kpe_buf.at[ci, i], sem.at[1, ci]).start()

    def wait_chunk(ci):
        pltpu.make_async_copy(ckv_hbm.at[pl.ds(0, pg)], ckv_buf.at[ci], sem.at[0, ci]).wait()
        pltpu.make_async_copy(kpe_hbm.at[pl.ds(0, pg)], kpe_buf.at[ci], sem.at[1, ci]).wait()

    @pl.when(s_id == 0)
    def _():
        for ci in range(nc):
            start_chunk(0, ci)

    nxt = jnp.minimum(s_id + 1, n_seq - 1)

    q = q_ref[0]
    q_pe = q[:, KV_LORA:KV_LORA + MLA_ROPE]
    q_rows = jnp.concatenate([q[:, :KV_LORA].astype(F32), jnp.zeros((LANES - Q_ROWS, KV_LORA), F32)], axis=0)
    q_t = q_rows.T.astype(BF16)
    group_keys = DEC_GROUP * PAGE_SIZE
    n_groups = pg // DEC_GROUP
    m = jnp.full((Q_ROWS, 1), NEG_BIG, F32)
    l = jnp.zeros((Q_ROWS, 1), F32)
    acc = jnp.zeros((Q_ROWS, KV_LORA), F32)
    for ci in range(nc):
        wait_chunk(ci)
        ckv_groups, parts = [], []
        for g in range(n_groups):
            ckv_g = ckv_buf[ci, g * DEC_GROUP:(g + 1) * DEC_GROUP].reshape(group_keys, KV_LORA).astype(BF16)
            ckv_groups.append(ckv_g)
            s_lat_t = _dot(ckv_g, q_t)
            for i in range(DEC_GROUP):
                s_lat = s_lat_t[i * PAGE_SIZE:(i + 1) * PAGE_SIZE].T[:Q_ROWS]
                kpe_t = kpe_buf[ci, g * DEC_GROUP + i].astype(BF16)
                parts.append(s_lat + _dot(q_pe, kpe_t))
        s = jnp.concatenate(parts, axis=-1)
        m_new = jnp.maximum(m, jnp.max(s, axis=-1, keepdims=True))
        corr = jnp.exp2(m - m_new)
        pb = jnp.exp2(s - m_new)
        l = l * corr + jnp.sum(pb, axis=-1, keepdims=True)
        pb = pb.astype(BF16)
        pv = _dot(pb[:, :group_keys], ckv_groups[0])
        for g in range(1, n_groups):
            pv = pv + _dot(pb[:, g * group_keys:(g + 1) * group_keys], ckv_groups[g])
        acc = acc * corr + pv
        m = m_new
        start_chunk(nxt, ci)

    s_self = jnp.sum(q.astype(F32) * kn_ref[0].astype(F32), axis=-1, keepdims=True)
    m_fin = jnp.maximum(m, s_self)
    corr = jnp.exp2(m - m_fin)
    p_self = jnp.exp2(s_self - m_fin)
    l_fin = l * corr + p_self
    o_ref[0] = (acc * corr + p_self * cn_ref[0]) * (1.0 / l_fin)

    @pl.when(s_id == n_seq - 1)
    def _():
        for ci in range(nc):
            wait_chunk(ci)


def _mla_decode(page_table, qd, kn, cn, cache_ckv, cache_kpe_t, pg):
    n_seq, n_pages = page_table.shape
    nc = n_pages // pg
    grid_spec = pltpu.PrefetchScalarGridSpec(
        num_scalar_prefetch=1,
        grid=(n_seq,),
        in_specs=[pl.BlockSpec((1, Q_ROWS, QK_WIDTH), lambda s, pt: (s, 0, 0)),
                  pl.BlockSpec((1, 1, QK_WIDTH), lambda s, pt: (s, 0, 0)),
                  pl.BlockSpec((1, 1, KV_LORA), lambda s, pt: (s, 0, 0)),
                  pl.BlockSpec(memory_space=pl.ANY),
                  pl.BlockSpec(memory_space=pl.ANY)],
        out_specs=pl.BlockSpec((1, Q_ROWS, KV_LORA), lambda s, pt: (s, 0, 0)),
        scratch_shapes=[pltpu.VMEM((nc, pg, PAGE_SIZE, KV_LORA), F32),
                        pltpu.VMEM((nc, pg, MLA_ROPE, PAGE_SIZE), F32),
                        pltpu.SemaphoreType.DMA((2, nc))])
    return pl.pallas_call(
        functools.partial(_mla_decode_kernel, pg=pg, nc=nc, n_seq=n_seq),
        grid_spec=grid_spec,
        out_shape=jax.ShapeDtypeStruct((n_seq, Q_ROWS, KV_LORA), F32),
        compiler_params=pltpu.CompilerParams(dimension_semantics=("arbitrary",)),
        name="mla_decode",
    )(page_table, qd, kn, cn, cache_ckv, cache_kpe_t)


def _ffn_kernel(x_ref, og_ref, ol_ref, wuv_ref, wout_ref, nffn_ref, wg_ref, wu_ref, wd_ref, nfin_ref, y_ref):
    om = [_dot(ol_ref[h], wuv_ref[h]).astype(BF16) for h in range(MLA_HEADS)]
    cat = jnp.concatenate([og_ref[...].astype(BF16)] + om, axis=-1)
    h1 = x_ref[...] + _dot(cat, wout_ref[...])
    n = _rms(h1, nffn_ref[...]).astype(BF16)
    gate = _dot(n, wg_ref[...])
    up = _dot(n, wu_ref[...])
    act = (gate * jax.nn.sigmoid(gate) * up).astype(BF16)
    h2 = h1 + _dot(act, wd_ref[...])
    y_ref[...] = _rms(h2, nfin_ref[...])


def _ffn(x, og, olat, w, tm):
    t = x.shape[0]
    row = lambda width: pl.BlockSpec((tm, width), lambda i: (i, 0))
    weights = (w["w_uv"], w["w_out"], w["norm_ffn"], w["w_gate"], w["w_up"], w["w_down"], w["norm_final"])
    return pl.pallas_call(
        _ffn_kernel,
        grid=(t // tm,),
        in_specs=[row(D_MODEL), row(512), pl.BlockSpec((MLA_HEADS, tm, KV_LORA), lambda i: (0, i, 0))]
                 + [_full_spec(a) for a in weights],
        out_specs=row(D_MODEL),
        out_shape=jax.ShapeDtypeStruct((t, D_MODEL), F32),
        compiler_params=pltpu.CompilerParams(dimension_semantics=("parallel",)),
        name="ffn",
    )(x, og, olat, *weights)


def _prep_weights(norm_mix, w_in, w_gk, b_gk, gla_norm, q_norm, kv_norm, w_uq, w_ukv, w_out,
                  norm_ffn, w_gate, w_up, w_down, norm_final):
    hk = GLA_HEADS * GLA_DK
    hv = GLA_HEADS * GLA_DV
    sizes = (hk, hk, hv, GLA_GATE_RANK, hv, Q_LORA, KV_LORA, MLA_ROPE)
    bounds = [0]
    for sz in sizes:
        bounds.append(bounds[-1] + sz)
    w_in_t = jnp.swapaxes(w_in[0], 0, 1)
    wq, wk, wv, wgr, wg, wcq, wckv, wkpe = (w_in_t[bounds[i]:bounds[i + 1]] for i in range(8))
    pad = jnp.zeros((_IN_PACKED - _OFF_TAIL - MLA_ROPE - GLA_GATE_RANK, D_MODEL), w_in.dtype)
    w_in_p = jnp.concatenate([wq, wk, wv, wg, wcq, wckv, wkpe, wgr, pad], axis=0).astype(BF16)

    w_gk_p = jnp.zeros((LANES, hk), F32).at[_TAIL_GR:_TAIL_GR + GLA_GATE_RANK].set(w_gk[0]).astype(BF16)

    wuq = w_uq[0].reshape(Q_LORA, MLA_HEADS, MLA_NOPE + MLA_ROPE)
    wuq_nope = wuq[..., :MLA_NOPE].reshape(Q_LORA, MLA_HEADS * MLA_NOPE)
    wuq_rope = jnp.pad(wuq[..., MLA_NOPE:], ((0, 0), (0, 0), (0, LANES - MLA_ROPE)))
    w_uq_p = jnp.concatenate([wuq_nope, wuq_rope.reshape(Q_LORA, MLA_HEADS * LANES)], axis=1).astype(BF16)

    wukv = w_ukv[0].reshape(KV_LORA, MLA_HEADS, MLA_NOPE + MLA_V)
    w_ukt = jnp.transpose(wukv[..., :MLA_NOPE], (1, 2, 0)).astype(BF16)
    w_uv = jnp.transpose(wukv[..., MLA_NOPE:], (1, 0, 2)).astype(BF16)

    r = lambda a: a.reshape(1, -1).astype(F32)
    return dict(norm_mix=r(norm_mix[0]), w_in=w_in_p, w_gk=w_gk_p, b_gk=r(b_gk[0]), q_norm=r(q_norm[0]),
                kv_norm=r(kv_norm[0]), w_uq=w_uq_p, w_ukt=w_ukt, w_uv=w_uv, gla_norm=r(gla_norm[0]),
                w_out=w_out[0].astype(BF16), norm_ffn=r(norm_ffn[0]), w_gate=w_gate[0].astype(BF16),
                w_up=w_up[0].astype(BF16), w_down=w_down[0].astype(BF16), norm_final=r(norm_final))


def _rope_tables(pos):
    half = MLA_ROPE // 2
    inv = ROPE_THETA ** (-jnp.arange(half, dtype=F32) / half)
    ang = pos.astype(F32)[:, None] * inv[None, :]
    cos, sin = jnp.cos(ang), jnp.sin(ang)
    z = jnp.zeros((pos.shape[0], LANES - MLA_ROPE), F32)
    return jnp.concatenate([cos, cos, z], axis=1), jnp.concatenate([-sin, sin, z], axis=1)


def kernel(x_prompt, x_sample, cache_ckv, cache_kpe, state_gla, page_table, meta_tokens, norm_mix, w_in, w_gk, b_gk, gla_norm, q_norm, kv_norm, w_uq, w_ukv, w_out, norm_ffn, w_gate, w_up, w_down, norm_final):
    n_b, seq, d = x_prompt.shape
    n_dec, t_dec, _ = x_sample.shape
    assert w_in.shape[0] == 1 and t_dec == 1 and d == D_MODEL
    n_pages = page_table.shape[1]
    past = n_pages * PAGE_SIZE
    hk = GLA_HEADS * GLA_DK
    w = _prep_weights(norm_mix, w_in, w_gk, b_gk, gla_norm, q_norm, kv_norm, w_uq, w_ukv, w_out,
                      norm_ffn, w_gate, w_up, w_down, norm_final)

    small_rows = 2 * LANES
    assert N_META <= LANES and n_dec == LANES
    xs = x_sample[:, 0]
    x_small = jnp.concatenate([meta_tokens.astype(F32), jnp.zeros((LANES - N_META, d), F32), xs], axis=0)
    pos_small = jnp.concatenate([jnp.arange(N_META), jnp.zeros((LANES - N_META,), jnp.int32),
                                 jnp.full((n_dec,), past, jnp.int32)])
    sm = _project(x_small, *_rope_tables(pos_small), w, small_rows)
    sm["ckv"], sm["kpe"] = sm["ckv"][0], sm["kpe"][0]
    xp = x_prompt.reshape(n_b * seq, d)
    pr = _project(xp, *_rope_tables(N_META + jnp.arange(seq)), w, 512, n_seq=n_b,
                  lead_ckv=sm["ckv"][:N_META], lead_kpe=sm["kpe"][:N_META])

    gla_in = ("gq", "gk", "lg", "gv", "g")
    _, s_meta = _gla_scan(*(sm[n][:N_META] for n in gla_in), w["gla_norm"],
                          jnp.zeros((1, hk, GLA_DV), F32), 1, GLA_STEP)
    og_p, s_prompt = _gla_scan(*(pr[n] for n in gla_in), w["gla_norm"], s_meta, n_b, 512)
    og_s, s_sample = _gla_token(*(sm[n][LANES:] for n in gla_in), w["gla_norm"],
                                state_gla[0].reshape(n_dec, hk, GLA_DV))

    kmeta = jnp.pad(sm["kcat"][:N_META], ((0, LANES - N_META), (0, 0)))
    olat_p = _mla_prompt(pr["qcat"], pr["kcat"], kmeta, n_b, 512)
    qd = jnp.pad(jnp.transpose(sm["qcat"][:, LANES:], (1, 0, 2)), ((0, 0), (0, Q_ROWS - MLA_HEADS), (0, 0)))
    cache_kpe_t = jnp.swapaxes(cache_kpe, 2, 3).reshape(-1, MLA_ROPE, PAGE_SIZE)
    o_dec = _mla_decode(page_table, qd, sm["kcat"][LANES:, None, :], sm["ckv"][LANES:, None, :],
                        cache_ckv.reshape(-1, PAGE_SIZE, KV_LORA), cache_kpe_t, 32)
    olat_s = jnp.transpose(o_dec[:, :MLA_HEADS], (1, 0, 2)).astype(BF16)

    y_prompt = _ffn(xp, og_p, olat_p, w, 512).reshape(n_b, seq, d)
    y_sample = _ffn(xs, og_s, olat_s, w, LANES).reshape(n_dec, 1, d)

    ckv_prompt = pr["ckv"][None]
    kpe_prompt = pr["kpe"][None]
    gla_prompt = s_prompt.reshape(1, n_b, GLA_HEADS, GLA_DK, GLA_DV)
    ckv_sample = sm["ckv"][LANES:].reshape(1, n_dec, 1, KV_LORA)
    kpe_sample = sm["kpe"][LANES:].reshape(1, n_dec, 1, MLA_ROPE)
    gla_sample = s_sample.reshape(1, n_dec, GLA_HEADS, GLA_DK, GLA_DV)
    return (y_prompt, y_sample, ckv_prompt, kpe_prompt, gla_prompt, ckv_sample, kpe_sample, gla_sample)
```

```python
import functools

import jax
import jax.numpy as jnp
from jax import lax
from jax.experimental import pallas as pl
from jax.experimental.pallas import tpu as pltpu

F32 = jnp.float32
BF16 = jnp.bfloat16

D_MODEL = 1024
N_META = 16
EPS = 1e-6
GLA_HEADS = 4
GLA_DV = 128
GLA_DK = 64
GLA_GATE_RANK = 16
GLA_GATE_NORM = 16.0
MLA_HEADS = 4
MLA_NOPE = 128
MLA_ROPE = 64
MLA_V = 128
Q_LORA = 384
KV_LORA = 256
ROPE_THETA = 10000.0
MLA_SCALE = (MLA_NOPE + MLA_ROPE) ** -0.5
PAGE_SIZE = 128
D_FF = 2816

LANES = 128
GLA_STEP = 16
GLA_CHUNK = 64
GLA_SAFE_LOG_DECAY = 40.0
QK_WIDTH = KV_LORA + LANES
NEG_BIG = -1e30
Q_PRESCALE = MLA_SCALE * 1.4426950408889634

_OFF_Q, _OFF_K, _OFF_V, _OFF_G, _OFF_CQ, _OFF_CKV, _OFF_TAIL, _IN_PACKED = (
    0, 256, 512, 1024, 1536, 1920, 2176, 2304)
_TAIL_GR = MLA_ROPE


def _rms(x, w):
    return x * lax.rsqrt(jnp.mean(x * x, axis=-1, keepdims=True) + EPS) * w


def _rope_tile(x, c, s):
    lane = lax.broadcasted_iota(jnp.int32, x.shape, 1)
    from_right = pltpu.roll(x, LANES - MLA_ROPE // 2, 1)
    from_left = pltpu.roll(x, MLA_ROPE // 2, 1)
    swapped = jnp.where(lane < MLA_ROPE // 2, from_right, from_left)
    return x * c + swapped * s


def _dot(a, b):
    return jnp.dot(a, b, preferred_element_type=F32)


def _dot_nt(a, b):
    return lax.dot_general(a, b, (((1,), (1,)), ((), ())), preferred_element_type=F32)


def _proj_kernel(x_ref, cos_ref, sin_ref, lead_ckv_ref, lead_kpe_ref, nmix_ref, win_ref, wgk_ref, bgk_ref,
                 qn_ref, kvn_ref, wuq_ref, wukt_ref,
                 gq_ref, gk_ref, gv_ref, lg_ref, g_ref, qcat_ref, kcat_ref, ckv_ref, kpe_ref, *, lead):
    tm = x_ref.shape[0]
    j = pl.program_id(1)
    n = _rms(x_ref[...], nmix_ref[...]).astype(BF16)
    proj = _dot_nt(n, win_ref[...])
    gq_ref[...] = proj[:, _OFF_Q:_OFF_K] * (GLA_DK ** -0.5)
    gk_ref[...] = proj[:, _OFF_K:_OFF_V]
    gv_ref[...] = proj[:, _OFF_V:_OFF_G]
    g_ref[...] = proj[:, _OFF_G:_OFF_CQ]
    tail = proj[:, _OFF_TAIL:_IN_PACKED]
    z = _dot(tail.astype(BF16), wgk_ref[...]) + bgk_ref[...]
    lg_ref[...] = jax.nn.log_sigmoid(z) * (1.0 / GLA_GATE_NORM)

    cos = cos_ref[...]
    sin = sin_ref[...]
    cq = _rms(proj[:, _OFF_CQ:_OFF_CKV], qn_ref[...]).astype(BF16)
    qh = _dot(cq, wuq_ref[...])
    for h in range(MLA_HEADS):
        q_nope = qh[:, h * MLA_NOPE:(h + 1) * MLA_NOPE].astype(BF16)
        q_lat = _dot(q_nope, wukt_ref[h])
        base = MLA_HEADS * MLA_NOPE + h * LANES
        q_pe = _rope_tile(qh[:, base:base + LANES], cos, sin)
        qcat_ref[h] = (jnp.concatenate([q_lat, q_pe], axis=-1) * Q_PRESCALE).astype(BF16)

    ckv = _rms(proj[:, _OFF_CKV:_OFF_TAIL], kvn_ref[...])
    k_pe = _rope_tile(tail, cos, sin)
    kcat_ref[...] = jnp.concatenate([ckv, k_pe], axis=-1).astype(BF16)

    if lead:
        @pl.when(j == 0)
        def _():
            ckv_ref[0, 0:lead, :] = lead_ckv_ref[...]
            kpe_ref[0, 0:lead, :] = lead_kpe_ref[...]
    rows = pl.ds(pl.multiple_of(lead + j * tm, 8), tm)
    ckv_ref[0, rows, :] = ckv
    kpe_ref[0, rows, :] = k_pe[:, :MLA_ROPE]


def _full_spec(a):
    nd = a.ndim
    return pl.BlockSpec(a.shape, lambda *_: (0,) * nd)


def _project(x, cos, sin, w, tm, n_seq=1, lead_ckv=None, lead_kpe=None):
    t = x.shape[0]
    tiles = t // n_seq // tm
    lead = 0 if lead_ckv is None else lead_ckv.shape[0]
    if lead == 0:
        lead_ckv, lead_kpe = jnp.zeros((8, KV_LORA), F32), jnp.zeros((8, MLA_ROPE), F32)
    seq_rows = lead + tiles * tm
    row = lambda width: pl.BlockSpec((tm, width), lambda b, j: (b * tiles + j, 0))
    pos = pl.BlockSpec((tm, LANES), lambda b, j: (j, 0))
    per_seq = lambda width: pl.BlockSpec((1, seq_rows, width), lambda b, j: (b, 0, 0))
    weights = (w["norm_mix"], w["w_in"], w["w_gk"], w["b_gk"], w["q_norm"], w["kv_norm"],
               w["w_uq"], w["w_ukt"])
    out_shape = (
        jax.ShapeDtypeStruct((t, 256), F32), jax.ShapeDtypeStruct((t, 256), F32),
        jax.ShapeDtypeStruct((t, 512), F32), jax.ShapeDtypeStruct((t, 256), F32),
        jax.ShapeDtypeStruct((t, 512), F32),
        jax.ShapeDtypeStruct((MLA_HEADS, t, QK_WIDTH), BF16),
        jax.ShapeDtypeStruct((t, QK_WIDTH), BF16),
        jax.ShapeDtypeStruct((n_seq, seq_rows, KV_LORA), F32),
        jax.ShapeDtypeStruct((n_seq, seq_rows, MLA_ROPE), F32))
    out_specs = (row(256), row(256), row(512), row(256), row(512),
                 pl.BlockSpec((MLA_HEADS, tm, QK_WIDTH), lambda b, j: (0, b * tiles + j, 0)),
                 row(QK_WIDTH), per_seq(KV_LORA), per_seq(MLA_ROPE))
    names = ("gq", "gk", "gv", "lg", "g", "qcat", "kcat", "ckv", "kpe")
    outs = pl.pallas_call(
        functools.partial(_proj_kernel, lead=lead),
        grid=(n_seq, tiles),
        in_specs=[row(D_MODEL), pos, pos, _full_spec(lead_ckv), _full_spec(lead_kpe)]
                 + [_full_spec(a) for a in weights],
        out_specs=out_specs,
        out_shape=out_shape,
        compiler_params=pltpu.CompilerParams(dimension_semantics=("parallel", "arbitrary")),
        name="project",
    )(x, cos, sin, lead_ckv, lead_kpe, *weights)
    return dict(zip(names, outs))


def _cum_log_decay(lg):
    n = lg.shape[0]
    tri = (lax.broadcasted_iota(jnp.int32, (n, n), 0) >= lax.broadcasted_iota(jnp.int32, (n, n), 1)).astype(F32)
    return jnp.dot(tri, lg, precision=lax.Precision.HIGHEST, preferred_element_type=F32)


def _head_masked_rows(x, pad_rows=0):
    lane_head = lax.broadcasted_iota(jnp.int32, x.shape, 1) // GLA_DK
    blocks = [jnp.where(lane_head == h, x, 0.0) for h in range(GLA_HEADS)]
    if pad_rows:
        blocks.append(jnp.zeros((pad_rows, x.shape[1]), x.dtype))
    return jnp.concatenate(blocks, axis=0)


def _head_stacked_values(v, pad_rows=0):
    blocks = [v[:, h * GLA_DV:(h + 1) * GLA_DV] for h in range(GLA_HEADS)]
    if pad_rows:
        blocks.append(jnp.zeros((pad_rows, GLA_DV), v.dtype))
    return jnp.concatenate(blocks, axis=0)


def _gla_state_update(state, kt, v, b_last, pad_rows=0):
    kbd = _head_masked_rows(kt, pad_rows)
    upd = _dot(kbd.T.astype(BF16), _head_stacked_values(v, pad_rows).astype(BF16))
    dec = jnp.broadcast_to(jnp.exp(b_last), (LANES, kt.shape[1])).T
    return dec * state + upd


def _gla_gate_out(o_heads, g, gn, dtype=BF16):
    normed = jnp.concatenate([_rms(o, gn) for o in o_heads], axis=-1)
    return (normed * (g * jax.nn.sigmoid(g))).astype(dtype)


def _gla_exact_step(state, q, k, lg, v, n_src=None):
    c = q.shape[0]
    n_src = c if n_src is None else n_src
    half = lax.broadcasted_iota(jnp.int32, (c, LANES), 1) // GLA_DK
    row_t = lax.broadcasted_iota(jnp.int32, q.shape, 0)
    b = jnp.broadcast_to(lg[0:1, :], lg.shape) if n_src == 1 else _cum_log_decay(lg)
    b_last = b[c - 1:c, :]

    o_st = _dot(_head_masked_rows(q * jnp.exp(b)).astype(BF16), state.astype(BF16))
    o_heads = [o_st[h * c:(h + 1) * c] for h in range(GLA_HEADS)]

    for s in range(n_src):
        decay = jnp.exp(jnp.minimum(b - b[s:s + 1, :], 0.0))
        w = jnp.where(row_t >= s, q * k[s:s + 1, :] * decay, 0.0)
        for h in range(GLA_HEADS):
            tile = w[:, (h // 2) * LANES:(h // 2 + 1) * LANES]
            a = jnp.sum(jnp.where(half == h % 2, tile, 0.0), axis=1, keepdims=True)
            o_heads[h] = o_heads[h] + a * v[s:s + 1, h * GLA_DV:(h + 1) * GLA_DV]

    state = _gla_state_update(state, k * jnp.exp(b_last - b), v, b_last, LANES - GLA_HEADS * c)
    return state, o_heads


def _gla_matmul_chunk(state, q, k, b, v):
    c = GLA_CHUNK
    b_last = b[c - 1:c, :]
    qm = _head_masked_rows(q * jnp.exp(b)).astype(BF16)
    k_inv = (k * jnp.exp(-b)).astype(BF16)
    a = _dot_nt(qm, k_inv)
    row_t = lax.broadcasted_iota(jnp.int32, a.shape, 0) % c
    col_s = lax.broadcasted_iota(jnp.int32, a.shape, 1)
    a = jnp.where(col_s <= row_t, a, 0.0).astype(BF16)
    o_st = _dot(qm, state.astype(BF16))
    vb = v.astype(BF16)
    o_heads = [o_st[h * c:(h + 1) * c] + _dot(a[h * c:(h + 1) * c], vb[:, h * GLA_DV:(h + 1) * GLA_DV])
               for h in range(GLA_HEADS)]
    state = _gla_state_update(state, k * jnp.exp(b_last - b), v, b_last)
    return state, o_heads


def _gla_kernel(q_ref, k_ref, lg_ref, v_ref, g_ref, gn_ref, s0_ref, og_ref, sout_ref, s_scr, *, tt):
    j = pl.program_id(1)

    @pl.when(j == 0)
    def _():
        s_scr[...] = s0_ref[0]

    gn = gn_ref[...]

    def exact_steps(start, n_steps):
        state = s_scr[...]
        for u in range(n_steps):
            sl = slice(start + u * GLA_STEP, start + (u + 1) * GLA_STEP)
            state, o_heads = _gla_exact_step(state, q_ref[sl, :], k_ref[sl, :], lg_ref[sl, :], v_ref[sl, :])
            og_ref[sl, :] = _gla_gate_out(o_heads, g_ref[sl, :], gn)
        s_scr[...] = state

    if tt % GLA_CHUNK == 0:
        chunks = [slice(ci * GLA_CHUNK, (ci + 1) * GLA_CHUNK) for ci in range(tt // GLA_CHUNK)]
        bs = [_cum_log_decay(lg_ref[sl, :]) for sl in chunks]
        worst = bs[0][GLA_CHUNK - 1:GLA_CHUNK, :]
        for b in bs[1:]:
            worst = jnp.minimum(worst, b[GLA_CHUNK - 1:GLA_CHUNK, :])
        mild = jnp.max(-worst) <= GLA_SAFE_LOG_DECAY

        @pl.when(mild)
        def _():
            state = s_scr[...]
            for sl, b in zip(chunks, bs):
                state, o_heads = _gla_matmul_chunk(state, q_ref[sl, :], k_ref[sl, :], b, v_ref[sl, :])
                og_ref[sl, :] = _gla_gate_out(o_heads, g_ref[sl, :], gn)
            s_scr[...] = state

        @pl.when(jnp.logical_not(mild))
        def _():
            exact_steps(0, tt // GLA_STEP)
    else:
        exact_steps(0, tt // GLA_STEP)

    @pl.when(j == pl.num_programs(1) - 1)
    def _():
        sout_ref[0] = s_scr[...]


def _gla_scan(q, k, lg, v, g, gla_norm, s0, n_seq, tt):
    t = q.shape[0]
    n_tiles = t // n_seq // tt
    row = lambda width: pl.BlockSpec((tt, width), lambda b, j: (b * n_tiles + j, 0))
    s0_map = (lambda b, j: (b, 0, 0)) if s0.shape[0] == n_seq else (lambda b, j: (0, 0, 0))
    hk = GLA_HEADS * GLA_DK
    og, s_out = pl.pallas_call(
        functools.partial(_gla_kernel, tt=tt),
        grid=(n_seq, n_tiles),
        in_specs=[row(256), row(256), row(256), row(512), row(512), _full_spec(gla_norm),
                  pl.BlockSpec((1, hk, GLA_DV), s0_map)],
        out_specs=(row(512), pl.BlockSpec((1, hk, GLA_DV), lambda b, j: (b, 0, 0))),
        out_shape=(jax.ShapeDtypeStruct((t, 512), BF16),
                   jax.ShapeDtypeStruct((n_seq, hk, GLA_DV), F32)),
        scratch_shapes=[pltpu.VMEM((hk, GLA_DV), F32)],
        compiler_params=pltpu.CompilerParams(dimension_semantics=("parallel", "arbitrary")),
        name="gla_scan",
    )(q, k, lg, v, g, gla_norm, s0)
    return og, s_out


TOKEN_ROWS = 8
TOKEN_SEQS = 8


def _gla_token_kernel(q_ref, k_ref, lg_ref, v_ref, g_ref, gn_ref, s0_ref, og_ref, sout_ref):
    gn = gn_ref[...]
    first = lax.broadcasted_iota(jnp.int32, (TOKEN_ROWS, 1), 0) == 0
    tile = lambda ref, i: jnp.where(first, ref[i:i + 1, :], 0.0)
    rows = []
    for i in range(TOKEN_SEQS):
        state, o_heads = _gla_exact_step(s0_ref[i], tile(q_ref, i), tile(k_ref, i), tile(lg_ref, i),
                                         tile(v_ref, i), n_src=1)
        rows.append(_gla_gate_out(o_heads, tile(g_ref, i), gn, F32)[0:1])
        sout_ref[i] = state
    og_ref[...] = jnp.concatenate(rows, axis=0)


def _gla_token(q, k, lg, v, g, gla_norm, s0):
    n = q.shape[0]
    hk = GLA_HEADS * GLA_DK
    row = lambda width: pl.BlockSpec((TOKEN_SEQS, width), lambda i: (i, 0))
    st = pl.BlockSpec((TOKEN_SEQS, hk, GLA_DV), lambda i: (i, 0, 0))
    return pl.pallas_call(
        _gla_token_kernel,
        grid=(n // TOKEN_SEQS,),
        in_specs=[row(256), row(256), row(256), row(512), row(512), _full_spec(gla_norm), st],
        out_specs=(row(512), st),
        out_shape=(jax.ShapeDtypeStruct((n, 512), F32),
                   jax.ShapeDtypeStruct((n, hk, GLA_DV), F32)),
        compiler_params=pltpu.CompilerParams(dimension_semantics=("parallel",)),
        name="gla_token",
    )(q, k, lg, v, g, gla_norm, s0)


def _mla_prompt_kernel(q_ref, k_ref, km_ref, o_ref, s_scr, sl_scr, mx_scr, l_scr, acc_scr, *, tq):
    qi = pl.program_id(1)
    m_rows = MLA_HEADS * tq
    q = q_ref[...].reshape(m_rows, QK_WIDTH)
    lane_fold = lambda a, op: functools.reduce(
        op, [a[:, i * LANES:(i + 1) * LANES] for i in range(a.shape[1] // LANES)])
    wide = lambda a, n: jnp.concatenate([a] * n, axis=-1)
    key_tile = lambda kt: k_ref[pl.ds(pl.multiple_of(kt * tq, tq), tq), :]
    last_keys = lambda: jnp.concatenate([key_tile(qi), km_ref[...]], axis=0)

    s = _dot_nt(q, last_keys())
    row = lax.broadcasted_iota(jnp.int32, s.shape, 0) % tq
    col = lax.broadcasted_iota(jnp.int32, s.shape, 1)
    last_visible = jnp.where(col < tq, row, tq + N_META - 1)
    s = jnp.where(col <= last_visible, s, NEG_BIG)
    sl_scr[...] = s
    mx_scr[...] = lane_fold(s, jnp.maximum)

    def score_body(kt, carry):
        s = _dot_nt(q, key_tile(kt))
        s_scr[kt] = s
        mx_scr[...] = jnp.maximum(mx_scr[...], lane_fold(s, jnp.maximum))
        return carry

    lax.fori_loop(0, qi, score_body, 0)
    m = jnp.broadcast_to(jnp.max(mx_scr[...], axis=-1, keepdims=True), (m_rows, LANES))

    p = jnp.exp2(sl_scr[...] - wide(m, tq // LANES + 1))
    l_scr[...] = lane_fold(p, jnp.add)
    acc_scr[...] = _dot(p.astype(BF16), last_keys()[:, :KV_LORA])
    m_wide = wide(m, tq // LANES)

    def value_body(kt, carry):
        p = jnp.exp2(s_scr[kt] - m_wide)
        l_scr[...] = l_scr[...] + lane_fold(p, jnp.add)
        acc_scr[...] = acc_scr[...] + _dot(p.astype(BF16), key_tile(kt)[:, :KV_LORA])
        return carry

    lax.fori_loop(0, qi, value_body, 0)
    inv_l = 1.0 / jnp.broadcast_to(jnp.sum(l_scr[...], axis=-1, keepdims=True), (m_rows, LANES))
    o = acc_scr[...] * wide(inv_l, KV_LORA // LANES)
    o_ref[...] = o.reshape(MLA_HEADS, tq, KV_LORA).astype(BF16)


def _mla_prompt(qcat, kcat, kmeta, n_seq, tq):
    t = kcat.shape[0]
    seq = t // n_seq
    nq = seq // tq
    m_rows = MLA_HEADS * tq
    return pl.pallas_call(
        functools.partial(_mla_prompt_kernel, tq=tq),
        grid=(n_seq, nq),
        in_specs=[pl.BlockSpec((MLA_HEADS, tq, QK_WIDTH), lambda b, i: (0, b * nq + i, 0)),
                  pl.BlockSpec((seq, QK_WIDTH), lambda b, i: (b, 0)),
                  _full_spec(kmeta)],
        out_specs=pl.BlockSpec((MLA_HEADS, tq, KV_LORA), lambda b, i: (0, b * nq + i, 0)),
        out_shape=jax.ShapeDtypeStruct((MLA_HEADS, t, KV_LORA), BF16),
        scratch_shapes=[pltpu.VMEM((max(nq - 1, 1), m_rows, tq), F32),
                        pltpu.VMEM((m_rows, tq + LANES), F32),
                        pltpu.VMEM((m_rows, LANES), F32),
                        pltpu.VMEM((m_rows, LANES), F32),
                        pltpu.VMEM((m_rows, KV_LORA), F32)],
        compiler_params=pltpu.CompilerParams(dimension_semantics=("parallel", "arbitrary")),
        name="mla_prompt",
    )(qcat, kcat, kmeta)


Q_ROWS = 8
DEC_GROUP = 8


def _mla_decode_kernel(pt_ref, q_ref, kn_ref, cn_ref, ckv_hbm, kpe_hbm, o_ref,
                       ckv_buf, kpe_buf, sem, *, pg, nc, n_seq):
    s_id = pl.program_id(0)

    def start_chunk(seq_i, ci):
        for i in range(pg):
            page = pt_ref[seq_i, ci * pg + i]
            pltpu.make_async_copy(ckv_hbm.at[page], ckv_buf.at[ci, i], sem.at[0, ci]).start()
            pltpu.make_async_copy(kpe_hbm.at[page], kpe_buf.at[ci, i], sem.at[1, ci]).start()

    def wait_chunk(ci):
        pltpu.make_async_copy(ckv_hbm.at[pl.ds(0, pg)], ckv_buf.at[ci], sem.at[0, ci]).wait()
        pltpu.make_async_copy(kpe_hbm.at[pl.ds(0, pg)], kpe_buf.at[ci], sem.at[1, ci]).wait()

    @pl.when(s_id == 0)
    def _():
        for ci in range(nc):
            start_chunk(0, ci)

    nxt = jnp.minimum(s_id + 1, n_seq - 1)

    q = q_ref[0]
    q_pe = q[:, KV_LORA:KV_LORA + MLA_ROPE]
    q_rows = jnp.concatenate([q[:, :KV_LORA].astype(F32), jnp.zeros((LANES - Q_ROWS, KV_LORA), F32)], axis=0)
    q_t = q_rows.T.astype(BF16)
    group_keys = DEC_GROUP * PAGE_SIZE
    n_groups = pg // DEC_GROUP
    m = jnp.full((Q_ROWS, 1), NEG_BIG, F32)
    l = jnp.zeros((Q_ROWS, 1), F32)
    acc = jnp.zeros((Q_ROWS, KV_LORA), F32)
    for ci in range(nc):
        wait_chunk(ci)
        ckv_groups, parts = [], []
        for g in range(n_groups):
            ckv_g = ckv_buf[ci, g * DEC_GROUP:(g + 1) * DEC_GROUP].reshape(group_keys, KV_LORA).astype(BF16)
            ckv_groups.append(ckv_g)
            s_lat_t = _dot(ckv_g, q_t)
            for i in range(DEC_GROUP):
                s_lat = s_lat_t[i * PAGE_SIZE:(i + 1) * PAGE_SIZE].T[:Q_ROWS]
                kpe_t = kpe_buf[ci, g * DEC_GROUP + i].astype(BF16)
                parts.append(s_lat + _dot(q_pe, kpe_t))
        s = jnp.concatenate(parts, axis=-1)
        m_new = jnp.maximum(m, jnp.max(s, axis=-1, keepdims=True))
        corr = jnp.exp2(m - m_new)
        pb = jnp.exp2(s - m_new)
        l = l * corr + jnp.sum(pb, axis=-1, keepdims=True)
        pb = pb.astype(BF16)
        pv = _dot(pb[:, :group_keys], ckv_groups[0])
        for g in range(1, n_groups):
            pv = pv + _dot(pb[:, g * group_keys:(g + 1) * group_keys], ckv_groups[g])
        acc = acc * corr + pv
        m = m_new
        start_chunk(nxt, ci)

    s_self = jnp.sum(q.astype(F32) * kn_ref[0].astype(F32), axis=-1, keepdims=True)
    m_fin = jnp.maximum(m, s_self)
    corr = jnp.exp2(m - m_fin)
    p_self = jnp.exp2(s_self - m_fin)
    l_fin = l * corr + p_self
    o_ref[0] = (acc * corr + p_self * cn_ref[0]) * (1.0 / l_fin)

    @pl.when(s_id == n_seq - 1)
    def _():
        for ci in range(nc):
            wait_chunk(ci)


def _mla_decode(page_table, qd, kn, cn, cache_ckv, cache_kpe_t, pg):
    n_seq, n_pages = page_table.shape
    nc = n_pages // pg
    grid_spec = pltpu.PrefetchScalarGridSpec(
        num_scalar_prefetch=1,
        grid=(n_seq,),
        in_specs=[pl.BlockSpec((1, Q_ROWS, QK_WIDTH), lambda s, pt: (s, 0, 0)),
                  pl.BlockSpec((1, 1, QK_WIDTH), lambda s, pt: (s, 0, 0)),
                  pl.BlockSpec((1, 1, KV_LORA), lambda s, pt: (s, 0, 0)),
                  pl.BlockSpec(memory_space=pl.ANY),
                  pl.BlockSpec(memory_space=pl.ANY)],
        out_specs=pl.BlockSpec((1, Q_ROWS, KV_LORA), lambda s, pt: (s, 0, 0)),
        scratch_shapes=[pltpu.VMEM((nc, pg, PAGE_SIZE, KV_LORA), F32),
                        pltpu.VMEM((nc, pg, MLA_ROPE, PAGE_SIZE), F32),
                        pltpu.SemaphoreType.DMA((2, nc))])
    return pl.pallas_call(
        functools.partial(_mla_decode_kernel, pg=pg, nc=nc, n_seq=n_seq),
        grid_spec=grid_spec,
        out_shape=jax.ShapeDtypeStruct((n_seq, Q_ROWS, KV_LORA), F32),
        compiler_params=pltpu.CompilerParams(dimension_semantics=("arbitrary",)),
        name="mla_decode",
    )(page_table, qd, kn, cn, cache_ckv, cache_kpe_t)


def _ffn_kernel(x_ref, og_ref, ol_ref, wuv_ref, wout_ref, nffn_ref, wg_ref, wu_ref, wd_ref, nfin_ref, y_ref):
    om = [_dot(ol_ref[h], wuv_ref[h]).astype(BF16) for h in range(MLA_HEADS)]
    cat = jnp.concatenate([og_ref[...].astype(BF16)] + om, axis=-1)
    h1 = x_ref[...] + _dot(cat, wout_ref[...])
    n = _rms(h1, nffn_ref[...]).astype(BF16)
    gate = _dot(n, wg_ref[...])
    up = _dot(n, wu_ref[...])
    act = (gate * jax.nn.sigmoid(gate) * up).astype(BF16)
    h2 = h1 + _dot(act, wd_ref[...])
    y_ref[...] = _rms(h2, nfin_ref[...])


def _ffn(x, og, olat, w, tm):
    t = x.shape[0]
    row = lambda width: pl.BlockSpec((tm, width), lambda i: (i, 0))
    weights = (w["w_uv"], w["w_out"], w["norm_ffn"], w["w_gate"], w["w_up"], w["w_down"], w["norm_final"])
    return pl.pallas_call(
        _ffn_kernel,
        grid=(t // tm,),
        in_specs=[row(D_MODEL), row(512), pl.BlockSpec((MLA_HEADS, tm, KV_LORA), lambda i: (0, i, 0))]
                 + [_full_spec(a) for a in weights],
        out_specs=row(D_MODEL),
        out_shape=jax.ShapeDtypeStruct((t, D_MODEL), F32),
        compiler_params=pltpu.CompilerParams(dimension_semantics=("parallel",)),
        name="ffn",
    )(x, og, olat, *weights)


def _prep_weights(norm_mix, w_in, w_gk, b_gk, gla_norm, q_norm, kv_norm, w_uq, w_ukv, w_out,
                  norm_ffn, w_gate, w_up, w_down, norm_final):
    hk = GLA_HEADS * GLA_DK
    hv = GLA_HEADS * GLA_DV
    sizes = (hk, hk, hv, GLA_GATE_RANK, hv, Q_LORA, KV_LORA, MLA_ROPE)
    bounds = [0]
    for sz in sizes:
        bounds.append(bounds[-1] + sz)
    w_in_t = jnp.swapaxes(w_in[0], 0, 1)
    wq, wk, wv, wgr, wg, wcq, wckv, wkpe = (w_in_t[bounds[i]:bounds[i + 1]] for i in range(8))
    pad = jnp.zeros((_IN_PACKED - _OFF_TAIL - MLA_ROPE - GLA_GATE_RANK, D_MODEL), w_in.dtype)
    w_in_p = jnp.concatenate([wq, wk, wv, wg, wcq, wckv, wkpe, wgr, pad], axis=0).astype(BF16)

    w_gk_p = jnp.zeros((LANES, hk), F32).at[_TAIL_GR:_TAIL_GR + GLA_GATE_RANK].set(w_gk[0]).astype(BF16)

    wuq = w_uq[0].reshape(Q_LORA, MLA_HEADS, MLA_NOPE + MLA_ROPE)
    wuq_nope = wuq[..., :MLA_NOPE].reshape(Q_LORA, MLA_HEADS * MLA_NOPE)
    wuq_rope = jnp.pad(wuq[..., MLA_NOPE:], ((0, 0), (0, 0), (0, LANES - MLA_ROPE)))
    w_uq_p = jnp.concatenate([wuq_nope, wuq_rope.reshape(Q_LORA, MLA_HEADS * LANES)], axis=1).astype(BF16)

    wukv = w_ukv[0].reshape(KV_LORA, MLA_HEADS, MLA_NOPE + MLA_V)
    w_ukt = jnp.transpose(wukv[..., :MLA_NOPE], (1, 2, 0)).astype(BF16)
    w_uv = jnp.transpose(wukv[..., MLA_NOPE:], (1, 0, 2)).astype(BF16)

    r = lambda a: a.reshape(1, -1).astype(F32)
    return dict(norm_mix=r(norm_mix[0]), w_in=w_in_p, w_gk=w_gk_p, b_gk=r(b_gk[0]), q_norm=r(q_norm[0]),
                kv_norm=r(kv_norm[0]), w_uq=w_uq_p, w_ukt=w_ukt, w_uv=w_uv, gla_norm=r(gla_norm[0]),
                w_out=w_out[0].astype(BF16), norm_ffn=r(norm_ffn[0]), w_gate=w_gate[0].astype(BF16),
                w_up=w_up[0].astype(BF16), w_down=w_down[0].astype(BF16), norm_final=r(norm_final))


def _rope_tables(pos):
    half = MLA_ROPE // 2
    inv = ROPE_THETA ** (-jnp.arange(half, dtype=F32) / half)
    ang = pos.astype(F32)[:, None] * inv[None, :]
    cos, sin = jnp.cos(ang), jnp.sin(ang)
    z = jnp.zeros((pos.shape[0], LANES - MLA_ROPE), F32)
    return jnp.concatenate([cos, cos, z], axis=1), jnp.concatenate([-sin, sin, z], axis=1)


def kernel(x_prompt, x_sample, cache_ckv, cache_kpe, state_gla, page_table, meta_tokens, norm_mix, w_in, w_gk, b_gk, gla_norm, q_norm, kv_norm, w_uq, w_ukv, w_out, norm_ffn, w_gate, w_up, w_down, norm_final):
    n_b, seq, d = x_prompt.shape
    n_dec, t_dec, _ = x_sample.shape
    assert w_in.shape[0] == 1 and t_dec == 1 and d == D_MODEL
    n_pages = page_table.shape[1]
    past = n_pages * PAGE_SIZE
    hk = GLA_HEADS * GLA_DK
    w = _prep_weights(norm_mix, w_in, w_gk, b_gk, gla_norm, q_norm, kv_norm, w_uq, w_ukv, w_out,
                      norm_ffn, w_gate, w_up, w_down, norm_final)

    small_rows = 2 * LANES
    assert N_META <= LANES and n_dec == LANES
    xs = x_sample[:, 0]
    x_small = jnp.concatenate([meta_tokens.astype(F32), jnp.zeros((LANES - N_META, d), F32), xs], axis=0)
    pos_small = jnp.concatenate([jnp.arange(N_META), jnp.zeros((LANES - N_META,), jnp.int32),
                                 jnp.full((n_dec,), past, jnp.int32)])
    sm = _project(x_small, *_rope_tables(pos_small), w, small_rows)
    sm["ckv"], sm["kpe"] = sm["ckv"][0], sm["kpe"][0]
    xp = x_prompt.reshape(n_b * seq, d)
    pr = _project(xp, *_rope_tables(N_META + jnp.arange(seq)), w, 512, n_seq=n_b,
                  lead_ckv=sm["ckv"][:N_META], lead_kpe=sm["kpe"][:N_META])

    gla_in = ("gq", "gk", "lg", "gv", "g")
    _, s_meta = _gla_scan(*(sm[n][:N_META] for n in gla_in), w["gla_norm"],
                          jnp.zeros((1, hk, GLA_DV), F32), 1, GLA_STEP)
    og_p, s_prompt = _gla_scan(*(pr[n] for n in gla_in), w["gla_norm"], s_meta, n_b, 512)
    og_s, s_sample = _gla_token(*(sm[n][LANES:] for n in gla_in), w["gla_norm"],
                                state_gla[0].reshape(n_dec, hk, GLA_DV))

    kmeta = jnp.pad(sm["kcat"][:N_META], ((0, LANES - N_META), (0, 0)))
    olat_p = _mla_prompt(pr["qcat"], pr["kcat"], kmeta, n_b, 512)
    qd = jnp.pad(jnp.transpose(sm["qcat"][:, LANES:], (1, 0, 2)), ((0, 0), (0, Q_ROWS - MLA_HEADS), (0, 0)))
    cache_kpe_t = jnp.swapaxes(cache_kpe, 2, 3).reshape(-1, MLA_ROPE, PAGE_SIZE)
    o_dec = _mla_decode(page_table, qd, sm["kcat"][LANES:, None, :], sm["ckv"][LANES:, None, :],
                        cache_ckv.reshape(-1, PAGE_SIZE, KV_LORA), cache_kpe_t, 32)
    olat_s = jnp.transpose(o_dec[:, :MLA_HEADS], (1, 0, 2)).astype(BF16)

    y_prompt = _ffn(xp, og_p, olat_p, w, 512).reshape(n_b, seq, d)
    y_sample = _ffn(xs, og_s, olat_s, w, LANES).reshape(n_dec, 1, d)

    ckv_prompt = pr["ckv"][None]
    kpe_prompt = pr["kpe"][None]
    gla_prompt = s_prompt.reshape(1, n_b, GLA_HEADS, GLA_DK, GLA_DV)
    ckv_sample = sm["ckv"][LANES:].reshape(1, n_dec, 1, KV_LORA)
    kpe_sample = sm["kpe"][LANES:].reshape(1, n_dec, 1, MLA_ROPE)
    gla_sample = s_sample.reshape(1, n_dec, GLA_HEADS, GLA_DK, GLA_DV)
    return (y_prompt, y_sample, ckv_prompt, kpe_prompt, gla_prompt, ckv_sample, kpe_sample, gla_sample)
```

```python
import functools

import jax
import jax.numpy as jnp
from jax import lax
from jax.experimental import pallas as pl
from jax.experimental.pallas import tpu as pltpu

F32 = jnp.float32
BF16 = jnp.bfloat16

D_MODEL = 1024
N_META = 16
EPS = 1e-6
GLA_HEADS = 4
GLA_DV = 128
GLA_DK = 64
GLA_GATE_RANK = 16
GLA_GATE_NORM = 16.0
MLA_HEADS = 4
MLA_NOPE = 128
MLA_ROPE = 64
MLA_V = 128
Q_LORA = 384
KV_LORA = 256
ROPE_THETA = 10000.0
MLA_SCALE = (MLA_NOPE + MLA_ROPE) ** -0.5
PAGE_SIZE = 128

LANES = 128
PROMPT_TILE = 512
DEC_CHUNK_PAGES = 32
GLA_STEP = 16
GLA_CHUNK = 64
GLA_SAFE_LOG_DECAY = 40.0
QK_WIDTH = KV_LORA + LANES
NEG_BIG = -1e30
Q_PRESCALE = MLA_SCALE * 1.4426950408889634

_OFF_Q, _OFF_K, _OFF_V, _OFF_G, _OFF_CQ, _OFF_CKV, _OFF_TAIL, _IN_PACKED = (
    0, 256, 512, 1024, 1536, 1920, 2176, 2304)
_TAIL_GR = MLA_ROPE


def _rms(x, w):
    return x * lax.rsqrt(jnp.mean(x * x, axis=-1, keepdims=True) + EPS) * w


def _rope_tile(x, c, s):
    lane = lax.broadcasted_iota(jnp.int32, x.shape, 1)
    from_right = pltpu.roll(x, LANES - MLA_ROPE // 2, 1)
    from_left = pltpu.roll(x, MLA_ROPE // 2, 1)
    swapped = jnp.where(lane < MLA_ROPE // 2, from_right, from_left)
    return x * c + swapped * s


def _dot(a, b):
    return jnp.dot(a, b, preferred_element_type=F32)


def _dot_nt(a, b):
    return lax.dot_general(a, b, (((1,), (1,)), ((), ())), preferred_element_type=F32)


def _proj_kernel(x_ref, cos_ref, sin_ref, lead_ckv_ref, lead_kpe_ref, nmix_ref, win_ref, wgk_ref, bgk_ref,
                 qn_ref, kvn_ref, wuq_ref, wukt_ref,
                 gq_ref, gk_ref, gv_ref, lg_ref, g_ref, qcat_ref, kcat_ref, ckv_ref, kpe_ref, *, lead):
    tm = x_ref.shape[0]
    j = pl.program_id(1)
    n = _rms(x_ref[...], nmix_ref[...]).astype(BF16)
    proj = _dot_nt(n, win_ref[...])
    gq_ref[...] = proj[:, _OFF_Q:_OFF_K] * (GLA_DK ** -0.5)
    gk_ref[...] = proj[:, _OFF_K:_OFF_V]
    gv_ref[...] = proj[:, _OFF_V:_OFF_G]
    g_ref[...] = proj[:, _OFF_G:_OFF_CQ]
    tail = proj[:, _OFF_TAIL:_IN_PACKED]
    z = _dot(tail.astype(BF16), wgk_ref[...]) + bgk_ref[...]
    lg_ref[...] = jax.nn.log_sigmoid(z) * (1.0 / GLA_GATE_NORM)

    cos = cos_ref[...]
    sin = sin_ref[...]
    cq = _rms(proj[:, _OFF_CQ:_OFF_CKV], qn_ref[...]).astype(BF16)
    qh = _dot(cq, wuq_ref[...])
    for h in range(MLA_HEADS):
        q_nope = qh[:, h * MLA_NOPE:(h + 1) * MLA_NOPE].astype(BF16)
        q_lat = _dot(q_nope, wukt_ref[h])
        base = MLA_HEADS * MLA_NOPE + h * LANES
        q_pe = _rope_tile(qh[:, base:base + LANES], cos, sin)
        qcat_ref[h] = (jnp.concatenate([q_lat, q_pe], axis=-1) * Q_PRESCALE).astype(BF16)

    ckv = _rms(proj[:, _OFF_CKV:_OFF_TAIL], kvn_ref[...])
    k_pe = _rope_tile(tail, cos, sin)
    kcat_ref[...] = jnp.concatenate([ckv, k_pe], axis=-1).astype(BF16)

    if lead:
        @pl.when(j == 0)
        def _():
            ckv_ref[0, 0:lead, :] = lead_ckv_ref[...]
            kpe_ref[0, 0:lead, :] = lead_kpe_ref[...]
    rows = pl.ds(pl.multiple_of(lead + j * tm, 8), tm)
    ckv_ref[0, rows, :] = ckv
    kpe_ref[0, rows, :] = k_pe[:, :MLA_ROPE]


def _full_spec(a):
    nd = a.ndim
    return pl.BlockSpec(a.shape, lambda *_: (0,) * nd)


def _project(x, cos, sin, w, tm, n_seq=1, lead_ckv=None, lead_kpe=None):
    t = x.shape[0]
    tiles = t // n_seq // tm
    lead = 0 if lead_ckv is None else lead_ckv.shape[0]
    if lead == 0:
        lead_ckv, lead_kpe = jnp.zeros((8, KV_LORA), F32), jnp.zeros((8, MLA_ROPE), F32)
    seq_rows = lead + tiles * tm
    row = lambda width: pl.BlockSpec((tm, width), lambda b, j: (b * tiles + j, 0))
    pos = pl.BlockSpec((tm, LANES), lambda b, j: (j, 0))
    per_seq = lambda width: pl.BlockSpec((1, seq_rows, width), lambda b, j: (b, 0, 0))
    weights = (w["norm_mix"], w["w_in"], w["w_gk"], w["b_gk"], w["q_norm"], w["kv_norm"],
               w["w_uq"], w["w_ukt"])
    out_shape = (
        jax.ShapeDtypeStruct((t, 256), F32), jax.ShapeDtypeStruct((t, 256), F32),
        jax.ShapeDtypeStruct((t, 512), F32), jax.ShapeDtypeStruct((t, 256), F32),
        jax.ShapeDtypeStruct((t, 512), F32),
        jax.ShapeDtypeStruct((MLA_HEADS, t, QK_WIDTH), BF16),
        jax.ShapeDtypeStruct((t, QK_WIDTH), BF16),
        jax.ShapeDtypeStruct((n_seq, seq_rows, KV_LORA), F32),
        jax.ShapeDtypeStruct((n_seq, seq_rows, MLA_ROPE), F32))
    out_specs = (row(256), row(256), row(512), row(256), row(512),
                 pl.BlockSpec((MLA_HEADS, tm, QK_WIDTH), lambda b, j: (0, b * tiles + j, 0)),
                 row(QK_WIDTH), per_seq(KV_LORA), per_seq(MLA_ROPE))
    names = ("gq", "gk", "gv", "lg", "g", "qcat", "kcat", "ckv", "kpe")
    outs = pl.pallas_call(
        functools.partial(_proj_kernel, lead=lead),
        grid=(n_seq, tiles),
        in_specs=[row(D_MODEL), pos, pos, _full_spec(lead_ckv), _full_spec(lead_kpe)]
                 + [_full_spec(a) for a in weights],
        out_specs=out_specs,
        out_shape=out_shape,
        compiler_params=pltpu.CompilerParams(dimension_semantics=("parallel", "arbitrary")),
        name="project",
    )(x, cos, sin, lead_ckv, lead_kpe, *weights)
    return dict(zip(names, outs))


def _cum_log_decay(lg):
    n = lg.shape[0]
    tri = (lax.broadcasted_iota(jnp.int32, (n, n), 0) >= lax.broadcasted_iota(jnp.int32, (n, n), 1)).astype(F32)
    return jnp.dot(tri, lg, precision=lax.Precision.HIGHEST, preferred_element_type=F32)


def _head_masked_rows(x, pad_rows=0):
    lane_head = lax.broadcasted_iota(jnp.int32, x.shape, 1) // GLA_DK
    blocks = [jnp.where(lane_head == h, x, 0.0) for h in range(GLA_HEADS)]
    if pad_rows:
        blocks.append(jnp.zeros((pad_rows, x.shape[1]), x.dtype))
    return jnp.concatenate(blocks, axis=0)


def _head_stacked_values(v, pad_rows=0):
    blocks = [v[:, h * GLA_DV:(h + 1) * GLA_DV] for h in range(GLA_HEADS)]
    if pad_rows:
        blocks.append(jnp.zeros((pad_rows, GLA_DV), v.dtype))
    return jnp.concatenate(blocks, axis=0)


def _gla_state_update(state, kt, v, b_last, pad_rows=0):
    kbd = _head_masked_rows(kt, pad_rows)
    upd = _dot(kbd.T.astype(BF16), _head_stacked_values(v, pad_rows).astype(BF16))
    dec = jnp.broadcast_to(jnp.exp(b_last), (LANES, kt.shape[1])).T
    return dec * state + upd


def _gla_gate_out(o_heads, g, gn, dtype=BF16):
    normed = jnp.concatenate([_rms(o, gn) for o in o_heads], axis=-1)
    return (normed * (g * jax.nn.sigmoid(g))).astype(dtype)


def _gla_exact_step(state, q, k, lg, v, n_src=None):
    c = q.shape[0]
    n_src = c if n_src is None else n_src
    half = lax.broadcasted_iota(jnp.int32, (c, LANES), 1) // GLA_DK
    row_t = lax.broadcasted_iota(jnp.int32, q.shape, 0)
    b = jnp.broadcast_to(lg[0:1, :], lg.shape) if n_src == 1 else _cum_log_decay(lg)
    b_last = b[c - 1:c, :]

    o_st = _dot(_head_masked_rows(q * jnp.exp(b)).astype(BF16), state.astype(BF16))
    o_heads = [o_st[h * c:(h + 1) * c] for h in range(GLA_HEADS)]

    for s in range(n_src):
        decay = jnp.exp(jnp.minimum(b - b[s:s + 1, :], 0.0))
        w = jnp.where(row_t >= s, q * k[s:s + 1, :] * decay, 0.0)
        for h in range(GLA_HEADS):
            tile = w[:, (h // 2) * LANES:(h // 2 + 1) * LANES]
            a = jnp.sum(jnp.where(half == h % 2, tile, 0.0), axis=1, keepdims=True)
            o_heads[h] = o_heads[h] + a * v[s:s + 1, h * GLA_DV:(h + 1) * GLA_DV]

    state = _gla_state_update(state, k * jnp.exp(b_last - b), v, b_last, LANES - GLA_HEADS * c)
    return state, o_heads


def _gla_matmul_chunk(state, q, k, b, v):
    c = GLA_CHUNK
    b_last = b[c - 1:c, :]
    qm = _head_masked_rows(q * jnp.exp(b)).astype(BF16)
    k_inv = (k * jnp.exp(-b)).astype(BF16)
    a = _dot_nt(qm, k_inv)
    row_t = lax.broadcasted_iota(jnp.int32, a.shape, 0) % c
    col_s = lax.broadcasted_iota(jnp.int32, a.shape, 1)
    a = jnp.where(col_s <= row_t, a, 0.0).astype(BF16)
    o_st = _dot(qm, state.astype(BF16))
    vb = v.astype(BF16)
    o_heads = [o_st[h * c:(h + 1) * c] + _dot(a[h * c:(h + 1) * c], vb[:, h * GLA_DV:(h + 1) * GLA_DV])
               for h in range(GLA_HEADS)]
    state = _gla_state_update(state, k * jnp.exp(b_last - b), v, b_last)
    return state, o_heads


def _gla_kernel(q_ref, k_ref, lg_ref, v_ref, g_ref, gn_ref, s0_ref, og_ref, sout_ref, s_scr, *, tt):
    j = pl.program_id(1)

    @pl.when(j == 0)
    def _():
        s_scr[...] = s0_ref[0]

    gn = gn_ref[...]

    def exact_steps(start, n_steps):
        state = s_scr[...]
        for u in range(n_steps):
            sl = slice(start + u * GLA_STEP, start + (u + 1) * GLA_STEP)
            state, o_heads = _gla_exact_step(state, q_ref[sl, :], k_ref[sl, :], lg_ref[sl, :], v_ref[sl, :])
            og_ref[sl, :] = _gla_gate_out(o_heads, g_ref[sl, :], gn)
        s_scr[...] = state

    if tt % GLA_CHUNK == 0:
        chunks = [slice(ci * GLA_CHUNK, (ci + 1) * GLA_CHUNK) for ci in range(tt // GLA_CHUNK)]
        bs = [_cum_log_decay(lg_ref[sl, :]) for sl in chunks]
        worst = bs[0][GLA_CHUNK - 1:GLA_CHUNK, :]
        for b in bs[1:]:
            worst = jnp.minimum(worst, b[GLA_CHUNK - 1:GLA_CHUNK, :])
        mild = jnp.max(-worst) <= GLA_SAFE_LOG_DECAY

        @pl.when(mild)
        def _():
            state = s_scr[...]
            for sl, b in zip(chunks, bs):
                state, o_heads = _gla_matmul_chunk(state, q_ref[sl, :], k_ref[sl, :], b, v_ref[sl, :])
                og_ref[sl, :] = _gla_gate_out(o_heads, g_ref[sl, :], gn)
            s_scr[...] = state

        @pl.when(jnp.logical_not(mild))
        def _():
            exact_steps(0, tt // GLA_STEP)
    else:
        exact_steps(0, tt // GLA_STEP)

    @pl.when(j == pl.num_programs(1) - 1)
    def _():
        sout_ref[0] = s_scr[...]


def _gla_scan(q, k, lg, v, g, gla_norm, s0, n_seq, tt):
    t = q.shape[0]
    n_tiles = t // n_seq // tt
    row = lambda width: pl.BlockSpec((tt, width), lambda b, j: (b * n_tiles + j, 0))
    s0_map = (lambda b, j: (b, 0, 0)) if s0.shape[0] == n_seq else (lambda b, j: (0, 0, 0))
    hk = GLA_HEADS * GLA_DK
    og, s_out = pl.pallas_call(
        functools.partial(_gla_kernel, tt=tt),
        grid=(n_seq, n_tiles),
        in_specs=[row(256), row(256), row(256), row(512), row(512), _full_spec(gla_norm),
                  pl.BlockSpec((1, hk, GLA_DV), s0_map)],
        out_specs=(row(512), pl.BlockSpec((1, hk, GLA_DV), lambda b, j: (b, 0, 0))),
        out_shape=(jax.ShapeDtypeStruct((t, 512), BF16),
                   jax.ShapeDtypeStruct((n_seq, hk, GLA_DV), F32)),
        scratch_shapes=[pltpu.VMEM((hk, GLA_DV), F32)],
        compiler_params=pltpu.CompilerParams(dimension_semantics=("parallel", "arbitrary")),
        name="gla_scan",
    )(q, k, lg, v, g, gla_norm, s0)
    return og, s_out


TOKEN_ROWS = 8
TOKEN_SEQS = 8


def _gla_token_kernel(q_ref, k_ref, lg_ref, v_ref, g_ref, gn_ref, s0_ref, og_ref, sout_ref):
    gn = gn_ref[...]
    first = lax.broadcasted_iota(jnp.int32, (TOKEN_ROWS, 1), 0) == 0
    tile = lambda ref, i: jnp.where(first, ref[i:i + 1, :], 0.0)
    rows = []
    for i in range(TOKEN_SEQS):
        state, o_heads = _gla_exact_step(s0_ref[i], tile(q_ref, i), tile(k_ref, i), tile(lg_ref, i),
                                         tile(v_ref, i), n_src=1)
        rows.append(_gla_gate_out(o_heads, tile(g_ref, i), gn, F32)[0:1])
        sout_ref[i] = state
    og_ref[...] = jnp.concatenate(rows, axis=0)


def _gla_token(q, k, lg, v, g, gla_norm, s0):
    n = q.shape[0]
    hk = GLA_HEADS * GLA_DK
    row = lambda width: pl.BlockSpec((TOKEN_SEQS, width), lambda i: (i, 0))
    st = pl.BlockSpec((TOKEN_SEQS, hk, GLA_DV), lambda i: (i, 0, 0))
    return pl.pallas_call(
        _gla_token_kernel,
        grid=(n // TOKEN_SEQS,),
        in_specs=[row(256), row(256), row(256), row(512), row(512), _full_spec(gla_norm), st],
        out_specs=(row(512), st),
        out_shape=(jax.ShapeDtypeStruct((n, 512), F32),
                   jax.ShapeDtypeStruct((n, hk, GLA_DV), F32)),
        compiler_params=pltpu.CompilerParams(dimension_semantics=("parallel",)),
        name="gla_token",
    )(q, k, lg, v, g, gla_norm, s0)


def _mla_prompt_kernel(q_ref, k_ref, km_ref, o_ref, s_scr, sl_scr, mx_scr, l_scr, acc_scr, *, tq):
    qi = pl.program_id(1)
    m_rows = MLA_HEADS * tq
    q = q_ref[...].reshape(m_rows, QK_WIDTH)
    lane_fold = lambda a, op: functools.reduce(
        op, [a[:, i * LANES:(i + 1) * LANES] for i in range(a.shape[1] // LANES)])
    wide = lambda a, n: jnp.concatenate([a] * n, axis=-1)
    key_tile = lambda kt: k_ref[pl.ds(pl.multiple_of(kt * tq, tq), tq), :]
    last_keys = lambda: jnp.concatenate([key_tile(qi), km_ref[...]], axis=0)

    s = _dot_nt(q, last_keys())
    row = lax.broadcasted_iota(jnp.int32, s.shape, 0) % tq
    col = lax.broadcasted_iota(jnp.int32, s.shape, 1)
    last_visible = jnp.where(col < tq, row, tq + N_META - 1)
    s = jnp.where(col <= last_visible, s, NEG_BIG)
    sl_scr[...] = s
    mx_scr[...] = lane_fold(s, jnp.maximum)

    def score_body(kt, carry):
        s = _dot_nt(q, key_tile(kt))
        s_scr[kt] = s
        mx_scr[...] = jnp.maximum(mx_scr[...], lane_fold(s, jnp.maximum))
        return carry

    lax.fori_loop(0, qi, score_body, 0)
    m = jnp.broadcast_to(jnp.max(mx_scr[...], axis=-1, keepdims=True), (m_rows, LANES))

    p = jnp.exp2(sl_scr[...] - wide(m, tq // LANES + 1))
    l_scr[...] = lane_fold(p, jnp.add)
    acc_scr[...] = _dot(p.astype(BF16), last_keys()[:, :KV_LORA])
    m_wide = wide(m, tq // LANES)

    def value_body(kt, carry):
        p = jnp.exp2(s_scr[kt] - m_wide)
        l_scr[...] = l_scr[...] + lane_fold(p, jnp.add)
        acc_scr[...] = acc_scr[...] + _dot(p.astype(BF16), key_tile(kt)[:, :KV_LORA])
        return carry

    lax.fori_loop(0, qi, value_body, 0)
    inv_l = 1.0 / jnp.broadcast_to(jnp.sum(l_scr[...], axis=-1, keepdims=True), (m_rows, LANES))
    o = acc_scr[...] * wide(inv_l, KV_LORA // LANES)
    o_ref[...] = o.reshape(MLA_HEADS, tq, KV_LORA).astype(BF16)


def _mla_prompt(qcat, kcat, kmeta, n_seq, tq):
    t = kcat.shape[0]
    seq = t // n_seq
    nq = seq // tq
    m_rows = MLA_HEADS * tq
    return pl.pallas_call(
        functools.partial(_mla_prompt_kernel, tq=tq),
        grid=(n_seq, nq),
        in_specs=[pl.BlockSpec((MLA_HEADS, tq, QK_WIDTH), lambda b, i: (0, b * nq + i, 0)),
                  pl.BlockSpec((seq, QK_WIDTH), lambda b, i: (b, 0)),
                  _full_spec(kmeta)],
        out_specs=pl.BlockSpec((MLA_HEADS, tq, KV_LORA), lambda b, i: (0, b * nq + i, 0)),
        out_shape=jax.ShapeDtypeStruct((MLA_HEADS, t, KV_LORA), BF16),
        scratch_shapes=[pltpu.VMEM((max(nq - 1, 1), m_rows, tq), F32),
                        pltpu.VMEM((m_rows, tq + LANES), F32),
                        pltpu.VMEM((m_rows, LANES), F32),
                        pltpu.VMEM((m_rows, LANES), F32),
                        pltpu.VMEM((m_rows, KV_LORA), F32)],
        compiler_params=pltpu.CompilerParams(dimension_semantics=("parallel", "arbitrary")),
        name="mla_prompt",
    )(qcat, kcat, kmeta)


Q_ROWS = 8
DEC_GROUP = 8


def _mla_decode_kernel(pt_ref, q_ref, kn_ref, cn_ref, ckv_hbm, kpe_hbm, o_ref,
                       ckv_buf, kpe_buf, sem, *, pg, nc, n_seq):
    s_id = pl.program_id(0)

    def start_chunk(seq_i, ci):
        for i in range(pg):
            page = pt_ref[seq_i, ci * pg + i]
            pltpu.make_async_copy(ckv_hbm.at[page], ckv_buf.at[ci, i], sem.at[0, ci]).start()
            pltpu.make_async_copy(kpe_hbm.at[page], kpe_buf.at[ci, i], sem.at[1, ci]).start()

    def wait_chunk(ci):
        pltpu.make_async_copy(ckv_hbm.at[pl.ds(0, pg)], ckv_buf.at[ci], sem.at[0, ci]).wait()
        pltpu.make_async_copy(kpe_hbm.at[pl.ds(0, pg)], kpe_buf.at[ci], sem.at[1, ci]).wait()

    @pl.when(s_id == 0)
    def _():
        for ci in range(nc):
            start_chunk(0, ci)

    nxt = jnp.minimum(s_id + 1, n_seq - 1)

    q = q_ref[0]
    q_pe = q[:, KV_LORA:KV_LORA + MLA_ROPE]
    q_rows = jnp.concatenate([q[:, :KV_LORA].astype(F32), jnp.zeros((LANES - Q_ROWS, KV_LORA), F32)], axis=0)
    q_t = q_rows.T.astype(BF16)
    group_keys = DEC_GROUP * PAGE_SIZE
    n_groups = pg // DEC_GROUP
    m = jnp.full((Q_ROWS, 1), NEG_BIG, F32)
    l = jnp.zeros((Q_ROWS, 1), F32)
    acc = jnp.zeros((Q_ROWS, KV_LORA), F32)
    for ci in range(nc):
        wait_chunk(ci)
        ckv_groups, parts = [], []
        for g in range(n_groups):
            ckv_g = ckv_buf[ci, g * DEC_GROUP:(g + 1) * DEC_GROUP].reshape(group_keys, KV_LORA).astype(BF16)
            ckv_groups.append(ckv_g)
            s_lat_t = _dot(ckv_g, q_t)
            for i in range(DEC_GROUP):
                s_lat = s_lat_t[i * PAGE_SIZE:(i + 1) * PAGE_SIZE].T[:Q_ROWS]
                kpe_t = kpe_buf[ci, g * DEC_GROUP + i].astype(BF16)
                parts.append(s_lat + _dot(q_pe, kpe_t))
        s = jnp.concatenate(parts, axis=-1)
        m_new = jnp.maximum(m, jnp.max(s, axis=-1, keepdims=True))
        corr = jnp.exp2(m - m_new)
        pb = jnp.exp2(s - m_new)
        l = l * corr + jnp.sum(pb, axis=-1, keepdims=True)
        pb = pb.astype(BF16)
        pv = _dot(pb[:, :group_keys], ckv_groups[0])
        for g in range(1, n_groups):
            pv = pv + _dot(pb[:, g * group_keys:(g + 1) * group_keys], ckv_groups[g])
        acc = acc * corr + pv
        m = m_new
        start_chunk(nxt, ci)

    s_self = jnp.sum(q.astype(F32) * kn_ref[0].astype(F32), axis=-1, keepdims=True)
    m_fin = jnp.maximum(m, s_self)
    corr = jnp.exp2(m - m_fin)
    p_self = jnp.exp2(s_self - m_fin)
    l_fin = l * corr + p_self
    o_ref[0] = (acc * corr + p_self * cn_ref[0]) * (1.0 / l_fin)

    @pl.when(s_id == n_seq - 1)
    def _():
        for ci in range(nc):
            wait_chunk(ci)


def _mla_decode(page_table, qd, kn, cn, cache_ckv, cache_kpe_t, pg):
    n_seq, n_pages = page_table.shape
    nc = n_pages // pg
    grid_spec = pltpu.PrefetchScalarGridSpec(
        num_scalar_prefetch=1,
        grid=(n_seq,),
        in_specs=[pl.BlockSpec((1, Q_ROWS, QK_WIDTH), lambda s, pt: (s, 0, 0)),
                  pl.BlockSpec((1, 1, QK_WIDTH), lambda s, pt: (s, 0, 0)),
                  pl.BlockSpec((1, 1, KV_LORA), lambda s, pt: (s, 0, 0)),
                  pl.BlockSpec(memory_space=pl.ANY),
                  pl.BlockSpec(memory_space=pl.ANY)],
        out_specs=pl.BlockSpec((1, Q_ROWS, KV_LORA), lambda s, pt: (s, 0, 0)),
        scratch_shapes=[pltpu.VMEM((nc, pg, PAGE_SIZE, KV_LORA), F32),
                        pltpu.VMEM((nc, pg, MLA_ROPE, PAGE_SIZE), F32),
                        pltpu.SemaphoreType.DMA((2, nc))])
    return pl.pallas_call(
        functools.partial(_mla_decode_kernel, pg=pg, nc=nc, n_seq=n_seq),
        grid_spec=grid_spec,
        out_shape=jax.ShapeDtypeStruct((n_seq, Q_ROWS, KV_LORA), F32),
        compiler_params=pltpu.CompilerParams(dimension_semantics=("arbitrary",)),
        name="mla_decode",
    )(page_table, qd, kn, cn, cache_ckv, cache_kpe_t)


def _ffn_kernel(x_ref, og_ref, ol_ref, wuv_ref, wout_ref, nffn_ref, wg_ref, wu_ref, wd_ref, nfin_ref, y_ref):
    om = [_dot(ol_ref[h], wuv_ref[h]).astype(BF16) for h in range(MLA_HEADS)]
    cat = jnp.concatenate([og_ref[...].astype(BF16)] + om, axis=-1)
    h1 = x_ref[...] + _dot(cat, wout_ref[...])
    n = _rms(h1, nffn_ref[...]).astype(BF16)
    gate = _dot(n, wg_ref[...])
    up = _dot(n, wu_ref[...])
    act = (gate * jax.nn.sigmoid(gate) * up).astype(BF16)
    h2 = h1 + _dot(act, wd_ref[...])
    y_ref[...] = _rms(h2, nfin_ref[...])


def _ffn(x, og, olat, w, tm):
    t = x.shape[0]
    row = lambda width: pl.BlockSpec((tm, width), lambda i: (i, 0))
    weights = (w["w_uv"], w["w_out"], w["norm_ffn"], w["w_gate"], w["w_up"], w["w_down"], w["norm_final"])
    return pl.pallas_call(
        _ffn_kernel,
        grid=(t // tm,),
        in_specs=[row(D_MODEL), row(512), pl.BlockSpec((MLA_HEADS, tm, KV_LORA), lambda i: (0, i, 0))]
                 + [_full_spec(a) for a in weights],
        out_specs=row(D_MODEL),
        out_shape=jax.ShapeDtypeStruct((t, D_MODEL), F32),
        compiler_params=pltpu.CompilerParams(dimension_semantics=("parallel",)),
        name="ffn",
    )(x, og, olat, *weights)


def _prep_weights(norm_mix, w_in, w_gk, b_gk, gla_norm, q_norm, kv_norm, w_uq, w_ukv, w_out,
                  norm_ffn, w_gate, w_up, w_down, norm_final):
    hk = GLA_HEADS * GLA_DK
    hv = GLA_HEADS * GLA_DV
    sizes = (hk, hk, hv, GLA_GATE_RANK, hv, Q_LORA, KV_LORA, MLA_ROPE)
    bounds = [0]
    for sz in sizes:
        bounds.append(bounds[-1] + sz)
    w_in_t = jnp.swapaxes(w_in[0], 0, 1)
    wq, wk, wv, wgr, wg, wcq, wckv, wkpe = (w_in_t[bounds[i]:bounds[i + 1]] for i in range(8))
    pad = jnp.zeros((_IN_PACKED - _OFF_TAIL - MLA_ROPE - GLA_GATE_RANK, D_MODEL), w_in.dtype)
    w_in_p = jnp.concatenate([wq, wk, wv, wg, wcq, wckv, wkpe, wgr, pad], axis=0).astype(BF16)

    w_gk_p = jnp.zeros((LANES, hk), F32).at[_TAIL_GR:_TAIL_GR + GLA_GATE_RANK].set(w_gk[0]).astype(BF16)

    wuq = w_uq[0].reshape(Q_LORA, MLA_HEADS, MLA_NOPE + MLA_ROPE)
    wuq_nope = wuq[..., :MLA_NOPE].reshape(Q_LORA, MLA_HEADS * MLA_NOPE)
    wuq_rope = jnp.pad(wuq[..., MLA_NOPE:], ((0, 0), (0, 0), (0, LANES - MLA_ROPE)))
    w_uq_p = jnp.concatenate([wuq_nope, wuq_rope.reshape(Q_LORA, MLA_HEADS * LANES)], axis=1).astype(BF16)

    wukv = w_ukv[0].reshape(KV_LORA, MLA_HEADS, MLA_NOPE + MLA_V)
    w_ukt = jnp.transpose(wukv[..., :MLA_NOPE], (1, 2, 0)).astype(BF16)
    w_uv = jnp.transpose(wukv[..., MLA_NOPE:], (1, 0, 2)).astype(BF16)

    r = lambda a: a.reshape(1, -1).astype(F32)
    return dict(norm_mix=r(norm_mix[0]), w_in=w_in_p, w_gk=w_gk_p, b_gk=r(b_gk[0]), q_norm=r(q_norm[0]),
                kv_norm=r(kv_norm[0]), w_uq=w_uq_p, w_ukt=w_ukt, w_uv=w_uv, gla_norm=r(gla_norm[0]),
                w_out=w_out[0].astype(BF16), norm_ffn=r(norm_ffn[0]), w_gate=w_gate[0].astype(BF16),
                w_up=w_up[0].astype(BF16), w_down=w_down[0].astype(BF16), norm_final=r(norm_final))


def _rope_tables(pos):
    half = MLA_ROPE // 2
    inv = ROPE_THETA ** (-jnp.arange(half, dtype=F32) / half)
    ang = pos.astype(F32)[:, None] * inv[None, :]
    cos, sin = jnp.cos(ang), jnp.sin(ang)
    z = jnp.zeros((pos.shape[0], LANES - MLA_ROPE), F32)
    return jnp.concatenate([cos, cos, z], axis=1), jnp.concatenate([-sin, sin, z], axis=1)


def kernel(x_prompt, x_sample, cache_ckv, cache_kpe, state_gla, page_table, meta_tokens, norm_mix, w_in, w_gk, b_gk, gla_norm, q_norm, kv_norm, w_uq, w_ukv, w_out, norm_ffn, w_gate, w_up, w_down, norm_final):
    n_b, seq, d = x_prompt.shape
    n_dec, t_dec, _ = x_sample.shape
    assert w_in.shape[0] == 1 and t_dec == 1 and d == D_MODEL
    n_pages = page_table.shape[1]
    past = n_pages * PAGE_SIZE
    hk = GLA_HEADS * GLA_DK
    w = _prep_weights(norm_mix, w_in, w_gk, b_gk, gla_norm, q_norm, kv_norm, w_uq, w_ukv, w_out,
                      norm_ffn, w_gate, w_up, w_down, norm_final)

    small_rows = 2 * LANES
    assert N_META <= LANES and n_dec == LANES
    xs = x_sample[:, 0]
    x_small = jnp.concatenate([meta_tokens.astype(F32), jnp.zeros((LANES - N_META, d), F32), xs], axis=0)
    pos_small = jnp.concatenate([jnp.arange(N_META), jnp.zeros((LANES - N_META,), jnp.int32),
                                 jnp.full((n_dec,), past, jnp.int32)])
    sm = _project(x_small, *_rope_tables(pos_small), w, small_rows)
    sm["ckv"], sm["kpe"] = sm["ckv"][0], sm["kpe"][0]
    xp = x_prompt.reshape(n_b * seq, d)
    pr = _project(xp, *_rope_tables(N_META + jnp.arange(seq)), w, PROMPT_TILE, n_seq=n_b,
                  lead_ckv=sm["ckv"][:N_META], lead_kpe=sm["kpe"][:N_META])

    gla_in = ("gq", "gk", "lg", "gv", "g")
    _, s_meta = _gla_scan(*(sm[n][:N_META] for n in gla_in), w["gla_norm"],
                          jnp.zeros((1, hk, GLA_DV), F32), 1, GLA_STEP)
    og_p, s_prompt = _gla_scan(*(pr[n] for n in gla_in), w["gla_norm"], s_meta, n_b, PROMPT_TILE)
    og_s, s_sample = _gla_token(*(sm[n][LANES:] for n in gla_in), w["gla_norm"],
                                state_gla[0].reshape(n_dec, hk, GLA_DV))

    kmeta = jnp.pad(sm["kcat"][:N_META], ((0, LANES - N_META), (0, 0)))
    olat_p = _mla_prompt(pr["qcat"], pr["kcat"], kmeta, n_b, PROMPT_TILE)
    qd = jnp.pad(jnp.transpose(sm["qcat"][:, LANES:], (1, 0, 2)), ((0, 0), (0, Q_ROWS - MLA_HEADS), (0, 0)))
    cache_kpe_t = jnp.swapaxes(cache_kpe, 2, 3).reshape(-1, MLA_ROPE, PAGE_SIZE)
    o_dec = _mla_decode(page_table, qd, sm["kcat"][LANES:, None, :], sm["ckv"][LANES:, None, :],
                        cache_ckv.reshape(-1, PAGE_SIZE, KV_LORA), cache_kpe_t, DEC_CHUNK_PAGES)
    olat_s = jnp.transpose(o_dec[:, :MLA_HEADS], (1, 0, 2)).astype(BF16)

    y_prompt = _ffn(xp, og_p, olat_p, w, PROMPT_TILE).reshape(n_b, seq, d)
    y_sample = _ffn(xs, og_s, olat_s, w, LANES).reshape(n_dec, 1, d)

    ckv_prompt = pr["ckv"][None]
    kpe_prompt = pr["kpe"][None]
    gla_prompt = s_prompt.reshape(1, n_b, GLA_HEADS, GLA_DK, GLA_DV)
    ckv_sample = sm["ckv"][LANES:].reshape(1, n_dec, 1, KV_LORA)
    kpe_sample = sm["kpe"][LANES:].reshape(1, n_dec, 1, MLA_ROPE)
    gla_sample = s_sample.reshape(1, n_dec, GLA_HEADS, GLA_DK, GLA_DV)
    return (y_prompt, y_sample, ckv_prompt, kpe_prompt, gla_prompt, ckv_sample, kpe_sample, gla_sample)
```
